```python
import functools
import jax, jax.numpy as jnp
from jax import lax
import numpy as np

D_MODEL = 4096
BATCH = 4
SEQ = 4096
DEPTH = 4
DEC_BATCH = 8
DEC_SEQ = 16
PAST_LEN = 4096

CHUNK = 64
N_LEFT_CHUNKS = 8
BAND = N_LEFT_CHUNKS * CHUNK
D_A = 3 * D_MODEL // 8
DH_A = 128
H_A = D_A // DH_A
MAX_REL = 256
D_B = 3 * D_MODEL // 8
DK_B = 128
H_B = D_B // DK_B
DV_B = D_B // H_B
HGRN_BLOCK = CHUNK
D_C = D_MODEL // 4
H_C = 4
DH_C = D_C // H_C
N_MEM = 256
D_MIX = D_A + D_B + D_C
D_IN = 4 * D_A + 4 * D_B + 2 * D_C + 3 * D_MODEL
EPS = 1e-6
NEG = -1e30

kernel_name = 'hybrid_chunk_stream_encoder_step'


def _rmsnorm(x, g):
    xf = x.astype(jnp.float32)
    y = xf * lax.rsqrt(jnp.mean(xf * xf, axis=-1, keepdims=True) + EPS)
    return (y * g.astype(jnp.float32)).astype(x.dtype)


def _heads(t, h):
    return t.reshape(t.shape[:-1] + (h, t.shape[-1] // h))


def _split_in(h):
    sizes = (D_A,) * 4 + (D_B,) * 4 + (D_C,) * 2 + (D_MODEL,) * 3
    points = np.cumsum(sizes)[:-1].tolist()
    return jnp.split(h, points, axis=-1)


def _band_attend(q, k, v, q_pos, k_pos, rel_tab):
    qc = q_pos // CHUNK
    kc = k_pos // CHUNK
    valid = ((k_pos[None, :] >= 0) & (kc[None, :] <= qc[:, None])
             & (kc[None, :] >= qc[:, None] - N_LEFT_CHUNKS))
    rel = jnp.clip(q_pos[:, None] - k_pos[None, :], -MAX_REL, MAX_REL) + MAX_REL
    bias = rel_tab.astype(jnp.float32)[:, rel]
    s = jnp.einsum('bqhd,bkhd->bhqk', q, k).astype(jnp.float32) * (DH_A ** -0.5) + bias
    p = jax.nn.softmax(jnp.where(valid, s, NEG), axis=-1)
    return jnp.einsum('bhqk,bkhd->bqhd', p.astype(v.dtype), v)


def _attend_prompt(q, k, v, rel_tab):
    B, T, H, dh = q.shape
    nc = T // CHUNK
    padw = ((0, 0), (BAND, 0), (0, 0), (0, 0))
    kp = jnp.pad(k, padw)
    vp = jnp.pad(v, padw)
    qb = q.reshape(B, nc, CHUNK, H, dh).swapaxes(0, 1)

    def one(args):
        c, qch = args
        start = c * CHUNK
        kb = lax.dynamic_slice_in_dim(kp, start, BAND + CHUNK, axis=1)
        vb = lax.dynamic_slice_in_dim(vp, start, BAND + CHUNK, axis=1)
        q_pos = start + jnp.arange(CHUNK)
        k_pos = start - BAND + jnp.arange(BAND + CHUNK)
        return _band_attend(qch, kb, vb, q_pos, k_pos, rel_tab)

    out = lax.map(one, (jnp.arange(nc), qb))
    return out.swapaxes(0, 1).reshape(B, T, H, dh)


def _attend_sample(q, k, v, rel_tab, k_cache, v_cache):
    T = q.shape[1]
    ca = k_cache.shape[1]
    kk = jnp.concatenate([k_cache.astype(k.dtype), k], axis=1)
    vv = jnp.concatenate([v_cache.astype(v.dtype), v], axis=1)
    q_pos = PAST_LEN + jnp.arange(T)
    k_pos = jnp.concatenate([PAST_LEN - ca + jnp.arange(ca), q_pos])
    return _band_attend(q, kk, vv, q_pos, k_pos, rel_tab)


def _hgrn_forget(f_raw, lb):
    f_raw = f_raw.astype(jnp.float32)
    lb = lb.astype(jnp.float32)
    kf = (1.0 - lb) * jax.nn.sigmoid(-f_raw)
    logf = jnp.log1p(-kf)
    return logf, kf


def _hgrn2_recurrence(q, logf, k, v, s0):
    B, T, H, DK = q.shape
    DV = v.shape[-1]
    L = min(HGRN_BLOCK, T)
    pad = (-T) % L
    padw = ((0, 0), (0, pad), (0, 0), (0, 0))
    q, logf, k, v = (jnp.pad(t.astype(jnp.float32), padw) for t in (q, logf, k, v))
    n = (T + pad) // L

    def blocks(t):
        return t.reshape(B, n, L, H, t.shape[-1]).swapaxes(0, 1)

    causal = jnp.tril(jnp.ones((L, L), bool))[None, :, :, None, None]

    def step(S, blk):
        qb, gb, kb, vb = blk
        b = jnp.cumsum(gb, axis=1)
        o_inter = jnp.einsum('blhk,bhkv->blhv', qb * jnp.exp(b), S)
        diff = b[:, :, None] - b[:, None, :]
        decay = jnp.where(causal, jnp.exp(jnp.where(causal, diff, 0.0)), 0.0)
        a = jnp.einsum('bthk,btshk,bshk->bhts', qb, decay, kb)
        o_intra = jnp.einsum('bhts,bshv->bthv', a, vb)
        b_last = b[:, -1]
        S = (jnp.exp(b_last)[..., None] * S
             + jnp.einsum('bshk,bshv->bhkv', kb * jnp.exp(b_last[:, None] - b), vb))
        return S, o_inter + o_intra

    S, o = lax.scan(step, s0.astype(jnp.float32), (blocks(q), blocks(logf), blocks(k), blocks(v)))
    o = o.swapaxes(0, 1).reshape(B, n * L, H, DV)[:, :T]
    return o, S


def _mem_kv(mem, g, w):
    mk, mv = jnp.split(_rmsnorm(mem, g) @ w, 2, axis=-1)
    return _heads(mk, H_C), _heads(mv, H_C)


def _mem_attend(q, mk, mv):
    s = jnp.einsum('bthd,bmhd->bhtm', q, mk.astype(q.dtype)).astype(jnp.float32) * (DH_C ** -0.5)
    p = jax.nn.softmax(s, axis=-1)
    return jnp.einsum('bhtm,bmhd->bthd', p.astype(q.dtype), mv.astype(q.dtype))


def _layer(x, attend, s0, mk, mv, norm_g, w_in, rel_tab, lb, hgrn_g, w_branch, w_out):
    B, T, _ = x.shape
    xn = _rmsnorm(x, norm_g)
    qa, ka, va, za, qb, fb, ib, zb, qc, zc, ga, gb, gc = _split_in(xn @ w_in)
    ka = _heads(ka, H_A)
    va = _heads(va, H_A)
    ya = attend(_heads(qa, H_A), ka, va, rel_tab).reshape(B, T, D_A) * jax.nn.silu(za)
    logf, kf = _hgrn_forget(fb, lb)
    ob, s_new = _hgrn2_recurrence(_heads(jax.nn.silu(qb), H_B), _heads(logf, H_B),
                                  _heads(kf, H_B), _heads(ib, H_B), s0)
    ob = _rmsnorm(ob, hgrn_g.reshape(H_B, DV_B)).astype(x.dtype)
    yb = ob.reshape(B, T, D_B) * jax.nn.silu(zb)
    yc = _mem_attend(_heads(qc, H_C), mk, mv).reshape(B, T, D_C) * jax.nn.silu(zc)
    merged = (jax.nn.sigmoid(ga) * (ya @ w_branch[:D_A])
              + jax.nn.sigmoid(gb) * (yb @ w_branch[D_A:D_A + D_B])
              + jax.nn.sigmoid(gc) * (yc @ w_branch[D_A + D_B:]))
    return x + merged @ w_out, ka, va, s_new


def setup_inputs(seed: int = 0) -> dict:
    key = jax.random.key(seed)
    ks = jax.random.split(key, 20)
    f32 = jnp.float32
    ca = min(BAND, PAST_LEN)

    def nrm(k, shape, s=1.0):
        return jax.random.normal(k, shape, f32) * s

    return {
        'x_prompt': nrm(ks[0], (BATCH, SEQ, D_MODEL)),
        'x_sample': nrm(ks[1], (DEC_BATCH, DEC_SEQ, D_MODEL)),
        'mem_prompt': nrm(ks[2], (BATCH, N_MEM, D_MODEL)),
        'cache_attn_k': nrm(ks[3], (DEPTH, DEC_BATCH, ca, H_A, DH_A)),
        'cache_attn_v': nrm(ks[4], (DEPTH, DEC_BATCH, ca, H_A, DH_A)),
        'state_hgrn': nrm(ks[5], (DEPTH, DEC_BATCH, H_B, DK_B, DV_B), 0.5),
        'cache_mem_k': nrm(ks[6], (DEPTH, DEC_BATCH, N_MEM, H_C, DH_C)),
        'cache_mem_v': nrm(ks[7], (DEPTH, DEC_BATCH, N_MEM, H_C, DH_C)),
        'norm_gain': 1.0 + nrm(ks[8], (DEPTH, D_MODEL), 0.05),
        'w_in': nrm(ks[9], (DEPTH, D_MODEL, D_IN), D_MODEL ** -0.5),
        'rel_bias': nrm(ks[10], (DEPTH, H_A, 2 * MAX_REL + 1), 0.5),
        'lb_logits': nrm(ks[11], (DEPTH, D_B)),
        'hgrn_norm_gain': 1.0 + nrm(ks[12], (DEPTH, D_B), 0.05),
        'mem_norm_gain': 1.0 + nrm(ks[13], (DEPTH, D_MODEL), 0.05),
        'w_mem_kv': nrm(ks[14], (DEPTH, D_MODEL, 2 * D_C), D_MODEL ** -0.5),
        'w_branch': nrm(ks[15], (DEPTH, D_MIX, D_MODEL), D_MIX ** -0.5),
        'w_out': nrm(ks[16], (DEPTH, D_MODEL, D_MODEL), D_MODEL ** -0.5),
        'final_norm_gain': 1.0 + nrm(ks[17], (D_MODEL,), 0.05),
    }


def reference(x_prompt, x_sample, mem_prompt, cache_attn_k, cache_attn_v, state_hgrn,
              cache_mem_k, cache_mem_v, norm_gain, w_in, rel_bias, lb_logits,
              hgrn_norm_gain, mem_norm_gain, w_mem_kv, w_branch, w_out, final_norm_gain):
    sm = jax.nn.softmax(lb_logits.astype(jnp.float32), axis=0)
    lb_all = jnp.cumsum(sm, axis=0) - sm[0]
    ca_p = min(BAND, x_prompt.shape[1])
    s0_p = jnp.zeros((x_prompt.shape[0], H_B, DK_B, DV_B), jnp.float32)

    xp, xs = x_prompt, x_sample
    kp_l, vp_l, sp_l, mkp_l, mvp_l = [], [], [], [], []
    ks_l, vs_l, ss_l = [], [], []
    for l in range(DEPTH):
        shared = (norm_gain[l], w_in[l], rel_bias[l], lb_all[l], hgrn_norm_gain[l],
                  w_branch[l], w_out[l])
        mk, mv = _mem_kv(mem_prompt, mem_norm_gain[l], w_mem_kv[l])
        xp, ka, va, s_fin = _layer(xp, _attend_prompt, s0_p, mk, mv, *shared)
        kp_l.append(ka[:, -ca_p:])
        vp_l.append(va[:, -ca_p:])
        sp_l.append(s_fin)
        mkp_l.append(mk)
        mvp_l.append(mv)
        att_s = functools.partial(_attend_sample, k_cache=cache_attn_k[l], v_cache=cache_attn_v[l])
        xs, ka_s, va_s, s_new = _layer(xs, att_s, state_hgrn[l], cache_mem_k[l], cache_mem_v[l], *shared)
        ks_l.append(ka_s)
        vs_l.append(va_s)
        ss_l.append(s_new)

    y_prompt = _rmsnorm(xp, final_norm_gain)
    y_sample = _rmsnorm(xs, final_norm_gain)
    return (y_prompt, y_sample,
            jnp.stack(kp_l), jnp.stack(vp_l), jnp.stack(sp_l), jnp.stack(mkp_l), jnp.stack(mvp_l),
            jnp.stack(ks_l), jnp.stack(vs_l), jnp.stack(ss_l))
```

```python
import functools

import numpy as np
import jax
import jax.numpy as jnp
from jax import lax
from jax.experimental import pallas as pl
from jax.experimental.pallas import tpu as pltpu

F32 = jnp.float32
BF16 = jnp.bfloat16

PAST_LEN = 4096
CHUNK = 64
N_LEFT_CHUNKS = 8
BAND = N_LEFT_CHUNKS * CHUNK
DH_A = 128
MAX_REL = 256
DK_B = 128
H_C = 4
EPS = 1e-6
NEG = -1e30

LANE = 128
VMEM_LIMIT_BYTES = 56 * 1024 * 1024

ATT_TQ = 256
ATT_NKB = BAND // ATT_TQ + 1
HG_L = 64
HG_TB = 512

NT_DIMS = (((1,), (1,)), ((), ()))
TN_DIMS = (((0,), (0,)), ((), ()))


def _params(*sem):
    return pltpu.CompilerParams(dimension_semantics=sem, vmem_limit_bytes=VMEM_LIMIT_BYTES)


def _silu(z):
    return z * jax.nn.sigmoid(z)


def _rmsnorm_kernel(x_ref, g_ref, o_ref):
    x = x_ref[...].astype(F32)
    y = x * lax.rsqrt(jnp.mean(x * x, axis=-1, keepdims=True) + EPS)
    o_ref[...] = (y * g_ref[...]).astype(o_ref.dtype)


def _rmsnorm(x, g, out_dtype, tm):
    m, d = x.shape
    tm = min(tm, m)
    return pl.pallas_call(
        _rmsnorm_kernel,
        out_shape=jax.ShapeDtypeStruct((m, d), out_dtype),
        grid=(m // tm,),
        in_specs=[pl.BlockSpec((tm, d), lambda i: (i, 0)),
                  pl.BlockSpec((1, d), lambda i: (0, 0))],
        out_specs=pl.BlockSpec((tm, d), lambda i: (i, 0)),
        compiler_params=_params("parallel"),
        name="rmsnorm",
    )(x, g.reshape(1, d).astype(F32))


def _mm_kernel(a_ref, b_ref, o_ref):
    o_ref[...] = jnp.dot(a_ref[...], b_ref[...], preferred_element_type=F32).astype(o_ref.dtype)


def _mm_res_kernel(a_ref, b_ref, r_ref, o_ref):
    o_ref[...] = r_ref[...] + jnp.dot(a_ref[...], b_ref[...], preferred_element_type=F32)


def _matmul(a, b, out_dtype, tm, tn, res=None):
    m, k = a.shape
    n = b.shape[1]
    tm, tn = min(tm, m), min(tn, n)
    in_specs = [pl.BlockSpec((tm, k), lambda i, j: (i, 0)),
                pl.BlockSpec((k, tn), lambda i, j: (0, j))]
    args = [a, b]
    kern = _mm_kernel
    if res is not None:
        in_specs.append(pl.BlockSpec((tm, tn), lambda i, j: (i, j)))
        args.append(res)
        kern = _mm_res_kernel
    return pl.pallas_call(
        kern,
        out_shape=jax.ShapeDtypeStruct((m, n), out_dtype),
        grid=(m // tm, n // tn),
        in_specs=in_specs,
        out_specs=pl.BlockSpec((tm, tn), lambda i, j: (i, j)),
        compiler_params=_params("parallel", "arbitrary"),
        name="matmul_res" if res is not None else "matmul",
    )(*args)


def _merge_kernel(ya_ref, yb_ref, yc_ref, wa_ref, wb_ref, wc_ref, ga_ref, gb_ref, gc_ref, o_ref):
    def part(y_ref, w_ref, g_ref):
        gate = jax.nn.sigmoid(g_ref[...].astype(F32))
        return gate * jnp.dot(y_ref[...], w_ref[...], preferred_element_type=F32)

    o_ref[...] = (part(ya_ref, wa_ref, ga_ref) + part(yb_ref, wb_ref, gb_ref)
                  + part(yc_ref, wc_ref, gc_ref)).astype(o_ref.dtype)


def _merge(ya, yb, yc, h, w_branch, gate_col0, tm, tn):
    m, d_a = ya.shape
    d_b, d_c = yb.shape[1], yc.shape[1]
    d = w_branch.shape[1]
    tm = min(tm, m)
    assert d_a == d_b and (d_a + d_b) % d_c == 0 and gate_col0 % tn == 0 and d % tn == 0
    g0, gstep = gate_col0 // tn, d // tn
    return pl.pallas_call(
        _merge_kernel,
        out_shape=jax.ShapeDtypeStruct((m, d), BF16),
        grid=(m // tm, d // tn),
        in_specs=[pl.BlockSpec((tm, d_a), lambda i, j: (i, 0)),
                  pl.BlockSpec((tm, d_b), lambda i, j: (i, 0)),
                  pl.BlockSpec((tm, d_c), lambda i, j: (i, 0)),
                  pl.BlockSpec((d_a, tn), lambda i, j: (0, j)),
                  pl.BlockSpec((d_b, tn), lambda i, j: (1, j)),
                  pl.BlockSpec((d_c, tn), lambda i, j: ((d_a + d_b) // d_c, j)),
                  pl.BlockSpec((tm, tn), lambda i, j: (i, g0 + j)),
                  pl.BlockSpec((tm, tn), lambda i, j: (i, g0 + gstep + j)),
                  pl.BlockSpec((tm, tn), lambda i, j: (i, g0 + 2 * gstep + j))],
        out_specs=pl.BlockSpec((tm, tn), lambda i, j: (i, j)),
        compiler_params=_params("parallel", "arbitrary"),
        name="merge",
    )(ya, yb, yc, w_branch, w_branch, w_branch, h, h, h)


def _lower_bound_kernel(x_ref, o_ref):
    x = x_ref[...].astype(F32)
    e = jnp.exp(x - jnp.max(x, axis=0, keepdims=True))
    sm = e / jnp.sum(e, axis=0, keepdims=True)
    row = lax.broadcasted_iota(jnp.int32, x.shape, 0)
    acc = jnp.zeros_like(x)
    for i in range(1, x.shape[0]):
        acc = acc + jnp.where(row >= i, sm[i:i + 1, :], 0.0)
    o_ref[...] = acc


def _lower_bounds(lb_logits):
    return pl.pallas_call(
        _lower_bound_kernel,
        out_shape=jax.ShapeDtypeStruct(lb_logits.shape, F32),
        name="hgrn_lower_bound",
    )(lb_logits)


def _softmax_pv(scores, values, dh):
    m = jnp.max(scores[0], axis=-1, keepdims=True)
    for s in scores[1:]:
        m = jnp.maximum(m, jnp.max(s, axis=-1, keepdims=True))
    l = None
    o = None
    for s, v in zip(scores, values):
        p = jnp.exp(s - m)
        ps = jnp.sum(p, axis=-1, keepdims=True)
        pv = jnp.dot(p.astype(BF16), v, preferred_element_type=F32)
        l = ps if l is None else l + ps
        o = pv if o is None else o + pv
    return o / l


def _attn_prompt_kernel(q_ref, *refs, n_heads, dh, nkb, scale):
    k_refs = refs[:nkb]
    v_refs = refs[nkb:2 * nkb]
    z_ref, bias_ref, o_ref = refs[2 * nkb:]
    t = pl.program_id(1)
    for hd in range(n_heads):
        sl = slice(hd * dh, (hd + 1) * dh)
        q = q_ref[0, :, sl]
        scores = []
        for j in range(nkb):
            s = lax.dot_general(q, k_refs[j][0, :, sl], NT_DIMS, preferred_element_type=F32)
            s = s * scale + bias_ref[hd, j]
            if j < nkb - 1:
                s = jnp.where(t >= nkb - 1 - j, s, NEG)
            scores.append(s)
        o = _softmax_pv(scores, [v_refs[j][0, :, sl] for j in range(nkb)], dh)
        z = z_ref[0, :, sl].astype(F32)
        o_ref[0, :, sl] = (o * _silu(z)).astype(o_ref.dtype)


def _prompt_bias(rel_tab):
    r = np.arange(ATT_TQ)[:, None]
    c = np.arange(ATT_TQ)[None, :]
    out = []
    for j in range(ATT_NKB):
        back = (ATT_NKB - 1 - j) * ATT_TQ
        rel = np.clip(r - c + back, -MAX_REL, MAX_REL) + MAX_REL
        dchunk = (c - back) // CHUNK - r // CHUNK
        valid = (dchunk <= 0) & (dchunk >= -N_LEFT_CHUNKS)
        b = jnp.take(rel_tab.astype(F32), jnp.asarray(rel.reshape(-1)), axis=1).reshape(-1, ATT_TQ, ATT_TQ)
        out.append(b + jnp.asarray(np.where(valid, 0.0, NEG), F32)[None])
    return jnp.stack(out, axis=1)


def _attn_prompt(h3, bias, d_a):
    bsz, t, _ = h3.shape
    n_heads = d_a // DH_A
    nkb = ATT_NKB
    assert t % ATT_TQ == 0 and ATT_TQ % CHUNK == 0 and BAND % ATT_TQ == 0

    def kv_spec(col, j):
        return pl.BlockSpec((1, ATT_TQ, d_a),
                            lambda b, i: (b, jnp.maximum(i - (nkb - 1 - j), 0), col))

    in_specs = ([pl.BlockSpec((1, ATT_TQ, d_a), lambda b, i: (b, i, 0))]
                + [kv_spec(1, j) for j in range(nkb)]
                + [kv_spec(2, j) for j in range(nkb)]
                + [pl.BlockSpec((1, ATT_TQ, d_a), lambda b, i: (b, i, 3)),
                   pl.BlockSpec(bias.shape, lambda b, i: (0, 0, 0, 0))])
    return pl.pallas_call(
        functools.partial(_attn_prompt_kernel, n_heads=n_heads, dh=DH_A, nkb=nkb, scale=DH_A ** -0.5),
        out_shape=jax.ShapeDtypeStruct((bsz, t, d_a), BF16),
        grid=(bsz, t // ATT_TQ),
        in_specs=in_specs,
        out_specs=pl.BlockSpec((1, ATT_TQ, d_a), lambda b, i: (b, i, 0)),
        compiler_params=_params("parallel", "arbitrary"),
        name="attn_prompt",
    )(h3, *([h3] * (2 * nkb)), h3, bias)


def _attn_sample_kernel(q_ref, k_ref, v_ref, z_ref, kc_ref, vc_ref, bc_ref, bn_ref, o_ref, *, n_heads, dh, scale):
    for hd in range(n_heads):
        sl = slice(hd * dh, (hd + 1) * dh)
        q = q_ref[0, :, sl]
        s_c = lax.dot_general(q, kc_ref[0, :, sl].astype(BF16), NT_DIMS, preferred_element_type=F32)
        s_n = lax.dot_general(q, k_ref[0, :, sl], NT_DIMS, preferred_element_type=F32)
        scores = [s_c * scale + bc_ref[hd], s_n * scale + bn_ref[hd]]
        o = _softmax_pv(scores, [vc_ref[0, :, sl].astype(BF16), v_ref[0, :, sl]], dh)
        z = z_ref[0, :, sl].astype(F32)
        o_ref[0, :, sl] = (o * _silu(z)).astype(o_ref.dtype)


def _sample_bias(rel_tab, t, ca):
    q_pos = PAST_LEN + np.arange(t)
    k_pos = np.concatenate([PAST_LEN - ca + np.arange(ca), q_pos])
    qc, kc = q_pos // CHUNK, k_pos // CHUNK
    valid = (k_pos[None, :] >= 0) & (kc[None, :] <= qc[:, None]) & (kc[None, :] >= qc[:, None] - N_LEFT_CHUNKS)
    rel = np.clip(q_pos[:, None] - k_pos[None, :], -MAX_REL, MAX_REL) + MAX_REL
    b = jnp.take(rel_tab.astype(F32), jnp.asarray(rel.reshape(-1)), axis=1).reshape(-1, t, ca + t)
    b = b + jnp.asarray(np.where(valid, 0.0, NEG), F32)[None]
    return b[:, :, :ca], b[:, :, ca:]


def _attn_sample(h3, k_cache, v_cache, bias_c, bias_n, d_a):
    bsz, t, _ = h3.shape
    ca = k_cache.shape[1]
    n_heads = d_a // DH_A
    row = lambda col: pl.BlockSpec((1, t, d_a), lambda b: (b, 0, col))
    cache = pl.BlockSpec((1, ca, d_a), lambda b: (b, 0, 0))
    return pl.pallas_call(
        functools.partial(_attn_sample_kernel, n_heads=n_heads, dh=DH_A, scale=DH_A ** -0.5),
        out_shape=jax.ShapeDtypeStruct((bsz, t, d_a), BF16),
        grid=(bsz,),
        in_specs=[row(0), row(1), row(2), row(3), cache, cache,
                  pl.BlockSpec(bias_c.shape, lambda b: (0, 0, 0)),
                  pl.BlockSpec(bias_n.shape, lambda b: (0, 0, 0))],
        out_specs=pl.BlockSpec((1, t, d_a), lambda b: (b, 0, 0)),
        compiler_params=_params("parallel"),
        name="attn_sample",
    )(h3, h3, h3, h3, k_cache.reshape(bsz, ca, d_a), v_cache.reshape(bsz, ca, d_a), bias_c, bias_n)


def _mem_attn_kernel(q_ref, z_ref, mk_ref, mv_ref, o_ref, *, n_heads, dh, scale):
    for hd in range(n_heads):
        sl = slice(hd * dh, (hd + 1) * dh)
        s = lax.dot_general(q_ref[0, :, sl], mk_ref[0, :, sl].astype(BF16), NT_DIMS,
                            preferred_element_type=F32) * scale
        o = _softmax_pv([s], [mv_ref[0, :, sl].astype(BF16)], dh)
        z = z_ref[0, :, sl].astype(F32)
        o_ref[0, :, sl] = (o * _silu(z)).astype(o_ref.dtype)


def _mem_attn(h3, mk, mv, q_col0, tq):
    bsz, t, _ = h3.shape
    n_mem, d_c = mk.shape[1], mk.shape[2]
    tq = min(tq, t)
    assert q_col0 % d_c == 0
    qb = q_col0 // d_c
    dh = d_c // H_C
    return pl.pallas_call(
        functools.partial(_mem_attn_kernel, n_heads=H_C, dh=dh, scale=dh ** -0.5),
        out_shape=jax.ShapeDtypeStruct((bsz, t, d_c), BF16),
        grid=(bsz, t // tq),
        in_specs=[pl.BlockSpec((1, tq, d_c), lambda b, i: (b, i, qb)),
                  pl.BlockSpec((1, tq, d_c), lambda b, i: (b, i, qb + 1)),
                  pl.BlockSpec((1, n_mem, d_c), lambda b, i: (b, 0, 0)),
                  pl.BlockSpec((1, n_mem, d_c), lambda b, i: (b, 0, 0))],
        out_specs=pl.BlockSpec((1, tq, d_c), lambda b, i: (b, i, 0)),
        compiler_params=_params("parallel", "arbitrary"),
        name="mem_attn",
    )(h3, h3, mk, mv)


def _seg_cumsum(x, row, group):
    d = 1
    while d < group:
        x = x + jnp.where((row & (group - 1)) >= d, pltpu.roll(x, d, 0), 0.0)
        d *= 2
    return x


def _hgrn_masks():
    t = np.arange(HG_L)[:, None]
    s = np.arange(HG_L)[None, :]
    m16 = [(t // 16 == i) & (s < 16 * i) for i in range(1, 4)]
    m4 = [(t // 16 == s // 16) & ((t % 16) // 4 == i) & (s % 16 < 4 * i) for i in range(1, 4)]
    base = (t // 4 == s // 4) & (s <= t)
    m1 = np.concatenate(m16, axis=1).astype(np.float32)
    m2 = np.concatenate(m4 + [base], axis=1).astype(np.float32)
    return jnp.asarray(m1), jnp.asarray(m2)


def _hgrn_block(q, g, k, v, st, m1, m2, cs_ref):
    L = HG_L
    row = lax.broadcasted_iota(jnp.int32, (L, LANE), 0)
    c4 = _seg_cumsum(g, row, 4)
    c16 = _seg_cumsum(g, row, 16)
    cs_ref[0] = c16
    b = c16
    for j in range(1, L // 16):
        b = b + jnp.where(row >= 16 * j, cs_ref[0, 16 * j - 1:16 * j, :], 0.0)
    cs_ref[1] = b

    q4 = (q * jnp.exp(c4)).astype(BF16)
    q16 = (q * jnp.exp(c16)).astype(BF16)
    q64 = (q * jnp.exp(b)).astype(BF16)

    k16 = []
    for i in range(1, L // 16):
        ref_row = cs_ref[1, 16 * i - 1:16 * i, :]
        k16.append(k * jnp.exp(jnp.minimum(ref_row - b, 0.0)))
    k4 = []
    for i in range(1, 4):
        ref_rows = jnp.concatenate(
            [jnp.broadcast_to(cs_ref[0, 16 * j + 4 * i - 1:16 * j + 4 * i, :], (16, LANE))
             for j in range(L // 16)], axis=0)
        k4.append(k * jnp.exp(jnp.minimum(ref_rows - c16, 0.0)))
    k_base = k * jnp.exp(-c4)
    ks1 = jnp.concatenate(k16, axis=0).astype(BF16)
    ks2 = jnp.concatenate(k4 + [k_base], axis=0).astype(BF16)

    a1 = lax.dot_general(q16, ks1, NT_DIMS, preferred_element_type=F32) * m1
    a2 = lax.dot_general(q4, ks2, NT_DIMS, preferred_element_type=F32) * m2
    vb = v.astype(BF16)
    o = (jnp.dot(a1.astype(BF16), jnp.concatenate([vb] * 3, axis=0), preferred_element_type=F32)
         + jnp.dot(a2.astype(BF16), jnp.concatenate([vb] * 4, axis=0), preferred_element_type=F32)
         + lax.dot_general(q64, st.astype(BF16), NT_DIMS, preferred_element_type=F32))

    b_last = cs_ref[1, L - 1:L, :]
    kd = (k * jnp.exp(b_last - b)).astype(BF16)
    st_new = st * jnp.exp(b_last) + lax.dot_general(vb, kd, TN_DIMS, preferred_element_type=F32)
    return o, st_new


def _hgrn_kernel(q_ref, f_ref, i_ref, z_ref, lb_ref, gn_ref, s0_ref, m1_ref, m2_ref,
                 y_ref, s_out_ref, st_ref, cs_ref, *, rows, n_sub):
    tb = pl.program_id(2)

    @pl.when(tb == 0)
    def _():
        st_ref[...] = s0_ref[0, 0].astype(F32).T

    lb = lb_ref[0]
    gn = gn_ref[0]
    m1 = m1_ref[...]
    m2 = m2_ref[...]

    def pad(x):
        if rows == HG_L:
            return x
        return jnp.concatenate([x, jnp.zeros((HG_L - rows, LANE), F32)], axis=0)

    def body(i, st):
        r0 = pl.multiple_of(i * rows, rows)
        qr = q_ref[0, pl.ds(r0, rows), :].astype(F32)
        fr = f_ref[0, pl.ds(r0, rows), :].astype(F32)
        vr = i_ref[0, pl.ds(r0, rows), :].astype(F32)
        zr = z_ref[0, pl.ds(r0, rows), :].astype(F32)
        kf = (1.0 - lb) * jax.nn.sigmoid(-fr)
        g = jnp.log1p(-kf)
        o, st = _hgrn_block(pad(_silu(qr)), pad(g), pad(kf), pad(vr), st, m1, m2, cs_ref)
        o = o[:rows]
        on = o * lax.rsqrt(jnp.mean(o * o, axis=-1, keepdims=True) + EPS) * gn
        y_ref[0, pl.ds(r0, rows), :] = (on * _silu(zr)).astype(y_ref.dtype)
        return st

    st = lax.fori_loop(0, n_sub, body, st_ref[...])
    st_ref[...] = st

    @pl.when(tb == pl.num_programs(2) - 1)
    def _():
        s_out_ref[0, 0] = st.T


def _hgrn(h3, lb, gain, s0, masks, col0, d_b):
    bsz, t, _ = h3.shape
    n_heads = d_b // DK_B
    assert col0 % LANE == 0 and DK_B == LANE and s0.shape == (bsz, n_heads, DK_B, LANE)
    if t % HG_L == 0:
        rows, tb = HG_L, min(HG_TB, t)
    else:
        assert t < HG_L and t % 16 == 0
        rows, tb = t, t
    assert t % tb == 0
    c0 = col0 // LANE
    col = lambda seg: pl.BlockSpec((1, tb, LANE), lambda b, hd, i: (b, i, c0 + seg * n_heads + hd))
    per_head = pl.BlockSpec((1, 1, LANE), lambda b, hd, i: (hd, 0, 0))
    state = pl.BlockSpec((1, 1, DK_B, LANE), lambda b, hd, i: (b, hd, 0, 0))
    m1, m2 = masks
    y, s_new = pl.pallas_call(
        functools.partial(_hgrn_kernel, rows=rows, n_sub=tb // rows),
        out_shape=(jax.ShapeDtypeStruct((bsz, t, d_b), BF16),
                   jax.ShapeDtypeStruct(s0.shape, F32)),
        grid=(bsz, n_heads, t // tb),
        in_specs=[col(0), col(1), col(2), col(3), per_head, per_head, state,
                  pl.BlockSpec(m1.shape, lambda b, hd, i: (0, 0)),
                  pl.BlockSpec(m2.shape, lambda b, hd, i: (0, 0))],
        out_specs=(pl.BlockSpec((1, tb, LANE), lambda b, hd, i: (b, i, hd)), state),
        scratch_shapes=[pltpu.VMEM((LANE, DK_B), F32), pltpu.VMEM((2, HG_L, LANE), F32)],
        compiler_params=_params("parallel", "parallel", "arbitrary"),
        name="hgrn2",
    )(h3, h3, h3, h3, lb.reshape(n_heads, 1, DK_B), gain.reshape(n_heads, 1, LANE).astype(F32),
      s0, m1, m2)
    return y, s_new


def _layer(x, attend, s0, mk, mv, norm_g, w_in, lb, hgrn_g, w_branch, w_out, masks, dims):
    d_a, d_b, d_c = dims
    bsz, t, d = x.shape
    m = bsz * t
    x2 = x.reshape(m, d)
    xn = _rmsnorm(x2, norm_g, BF16, 256)
    h = _matmul(xn, w_in, BF16, 1024, 1024)
    h3 = h.reshape(bsz, t, -1)
    ya = attend(h3)
    yb, s_new = _hgrn(h3, lb, hgrn_g, s0, masks, 4 * d_a, d_b)
    yc = _mem_attn(h3, mk, mv, 4 * d_a + 4 * d_b, 512)
    merged = _merge(ya.reshape(m, d_a), yb.reshape(m, d_b), yc.reshape(m, d_c), h, w_branch,
                    4 * d_a + 4 * d_b + 2 * d_c, 1024, 512)
    x_new = _matmul(merged, w_out, F32, 1024, 512, res=x2).reshape(bsz, t, d)
    ka = h3[:, :, d_a:2 * d_a]
    va = h3[:, :, 2 * d_a:3 * d_a]
    return x_new, ka, va, s_new


def kernel(x_prompt, x_sample, mem_prompt, cache_attn_k, cache_attn_v, state_hgrn, cache_mem_k, cache_mem_v, norm_gain, w_in, rel_bias, lb_logits, hgrn_norm_gain, mem_norm_gain, w_mem_kv, w_branch, w_out, final_norm_gain):
    depth = w_in.shape[0]
    bp, tp, d = x_prompt.shape
    bs, ts, _ = x_sample.shape
    h_a = rel_bias.shape[1]
    d_a = h_a * DH_A
    d_b = lb_logits.shape[1]
    d_c = w_mem_kv.shape[2] // 2
    h_b = d_b // DK_B
    n_mem = mem_prompt.shape[1]
    dims = (d_a, d_b, d_c)
    ca_s = cache_attn_k.shape[2]
    ca_p = min(BAND, tp)

    lb_all = _lower_bounds(lb_logits)
    masks = _hgrn_masks()
    s0_p = jnp.zeros((bp, h_b, DK_B, d_b // h_b), F32)
    mem2 = mem_prompt.reshape(bp * n_mem, d)

    xp, xs = x_prompt, x_sample
    kp_l, vp_l, sp_l, mkp_l, mvp_l, ks_l, vs_l, ss_l = [], [], [], [], [], [], [], []
    for l in range(depth):
        w_in_l = w_in[l].astype(BF16)
        w_branch_l = w_branch[l].astype(BF16)
        w_out_l = w_out[l].astype(BF16)
        shared = (norm_gain[l], w_in_l, lb_all[l], hgrn_norm_gain[l], w_branch_l, w_out_l, masks, dims)

        memn = _rmsnorm(mem2, mem_norm_gain[l], BF16, 256)
        mkv = _matmul(memn, w_mem_kv[l].astype(BF16), F32, 1024, 1024)
        mk = mkv[:, :d_c].reshape(bp, n_mem, d_c)
        mv = mkv[:, d_c:].reshape(bp, n_mem, d_c)
        bias_p = _prompt_bias(rel_bias[l])
        xp, ka, va, s_fin = _layer(xp, lambda h3: _attn_prompt(h3, bias_p, d_a), s0_p, mk, mv, *shared)
        kp_l.append(ka[:, -ca_p:].astype(F32).reshape(bp, ca_p, h_a, DH_A))
        vp_l.append(va[:, -ca_p:].astype(F32).reshape(bp, ca_p, h_a, DH_A))
        sp_l.append(s_fin)
        mkp_l.append(mk.reshape(bp, n_mem, H_C, d_c // H_C))
        mvp_l.append(mv.reshape(bp, n_mem, H_C, d_c // H_C))

        bias_c, bias_n = _sample_bias(rel_bias[l], ts, ca_s)
        att_s = lambda h3: _attn_sample(h3, cache_attn_k[l], cache_attn_v[l], bias_c, bias_n, d_a)
        xs, ka_s, va_s, s_new = _layer(xs, att_s, state_hgrn[l],
                                       cache_mem_k[l].reshape(bs, n_mem, d_c),
                                       cache_mem_v[l].reshape(bs, n_mem, d_c), *shared)
        ks_l.append(ka_s.astype(F32).reshape(bs, ts, h_a, DH_A))
        vs_l.append(va_s.astype(F32).reshape(bs, ts, h_a, DH_A))
        ss_l.append(s_new)

    y_prompt = _rmsnorm(xp.reshape(bp * tp, d), final_norm_gain, F32, 256).reshape(bp, tp, d)
    y_sample = _rmsnorm(xs.reshape(bs * ts, d), final_norm_gain, F32, 256).reshape(bs, ts, d)
    return (y_prompt, y_sample,
            jnp.stack(kp_l), jnp.stack(vp_l), jnp.stack(sp_l), jnp.stack(mkp_l), jnp.stack(mvp_l),
            jnp.stack(ks_l), jnp.stack(vs_l), jnp.stack(ss_l))
```

```python
import functools

import numpy as np
import jax
import jax.numpy as jnp
from jax import lax
from jax.experimental import pallas as pl
from jax.experimental.pallas import tpu as pltpu

F32 = jnp.float32
BF16 = jnp.bfloat16

PAST_LEN = 4096
CHUNK = 64
N_LEFT_CHUNKS = 8
BAND = N_LEFT_CHUNKS * CHUNK
DH_A = 128
MAX_REL = 256
DK_B = 128
H_C = 4
EPS = 1e-6
NEG = -1e30

LANE = 128
VMEM_LIMIT_BYTES = 56 * 1024 * 1024

ATT_TQ = 256
ATT_NKB = BAND // ATT_TQ + 1
HG_L = 64
HG_TB = 512
HG_HP = 2
HG_UNROLL = 2

NT_DIMS = (((1,), (1,)), ((), ()))
TN_DIMS = (((0,), (0,)), ((), ()))


def _params(*sem):
    return pltpu.CompilerParams(dimension_semantics=sem, vmem_limit_bytes=VMEM_LIMIT_BYTES)


def _silu(z):
    return z * jax.nn.sigmoid(z)


def _rmsnorm_kernel(x_ref, g_ref, o_ref):
    x = x_ref[...].astype(F32)
    y = x * lax.rsqrt(jnp.mean(x * x, axis=-1, keepdims=True) + EPS)
    o_ref[...] = (y * g_ref[...]).astype(o_ref.dtype)


def _rmsnorm(x, g, out_dtype, tm):
    m, d = x.shape
    tm = min(tm, m)
    return pl.pallas_call(
        _rmsnorm_kernel,
        out_shape=jax.ShapeDtypeStruct((m, d), out_dtype),
        grid=(m // tm,),
        in_specs=[pl.BlockSpec((tm, d), lambda i: (i, 0)),
                  pl.BlockSpec((1, d), lambda i: (0, 0))],
        out_specs=pl.BlockSpec((tm, d), lambda i: (i, 0)),
        compiler_params=_params("parallel"),
        name="rmsnorm",
    )(x, g.reshape(1, d).astype(F32))


def _mm_kernel(a_ref, b_ref, o_ref):
    o_ref[...] = jnp.dot(a_ref[...], b_ref[...], preferred_element_type=F32).astype(o_ref.dtype)


def _mm_res_kernel(a_ref, b_ref, r_ref, o_ref):
    o_ref[...] = r_ref[...] + jnp.dot(a_ref[...], b_ref[...], preferred_element_type=F32)


def _matmul(a, w, layer, out_dtype, tm, tn, res=None):
    m, k = a.shape
    n = w.shape[2]
    tm, tn = min(tm, m), min(tn, n)
    in_specs = [pl.BlockSpec((tm, k), lambda i, j: (i, 0)),
                pl.BlockSpec((None, k, tn), lambda i, j: (layer, 0, j))]
    args = [a, w]
    kern = _mm_kernel
    if res is not None:
        in_specs.append(pl.BlockSpec((tm, tn), lambda i, j: (i, j)))
        args.append(res)
        kern = _mm_res_kernel
    return pl.pallas_call(
        kern,
        out_shape=jax.ShapeDtypeStruct((m, n), out_dtype),
        grid=(m // tm, n // tn),
        in_specs=in_specs,
        out_specs=pl.BlockSpec((tm, tn), lambda i, j: (i, j)),
        compiler_params=_params("parallel", "arbitrary"),
        name="matmul_res" if res is not None else "matmul",
    )(*args)


def _merge_kernel(ya_ref, yb_ref, yc_ref, wa_ref, wb_ref, wc_ref, ga_ref, gb_ref, gc_ref, o_ref):
    def part(y_ref, w_ref, g_ref):
        gate = jax.nn.sigmoid(g_ref[...].astype(F32))
        return gate * jnp.dot(y_ref[...], w_ref[...], preferred_element_type=F32)

    o_ref[...] = (part(ya_ref, wa_ref, ga_ref) + part(yb_ref, wb_ref, gb_ref)
                  + part(yc_ref, wc_ref, gc_ref)).astype(o_ref.dtype)


def _merge(ya, yb, yc, h, w_branch, layer, gate_col0, tm, tn):
    m, d_a = ya.shape
    d_b, d_c = yb.shape[1], yc.shape[1]
    d = w_branch.shape[2]
    tm = min(tm, m)
    assert d_a == d_b and (d_a + d_b) % d_c == 0 and gate_col0 % tn == 0 and d % tn == 0
    g0, gstep = gate_col0 // tn, d // tn
    return pl.pallas_call(
        _merge_kernel,
        out_shape=jax.ShapeDtypeStruct((m, d), BF16),
        grid=(m // tm, d // tn),
        in_specs=[pl.BlockSpec((tm, d_a), lambda i, j: (i, 0)),
                  pl.BlockSpec((tm, d_b), lambda i, j: (i, 0)),
                  pl.BlockSpec((tm, d_c), lambda i, j: (i, 0)),
                  pl.BlockSpec((None, d_a, tn), lambda i, j: (layer, 0, j)),
                  pl.BlockSpec((None, d_b, tn), lambda i, j: (layer, 1, j)),
                  pl.BlockSpec((None, d_c, tn), lambda i, j: (layer, (d_a + d_b) // d_c, j)),
                  pl.BlockSpec((tm, tn), lambda i, j: (i, g0 + j)),
                  pl.BlockSpec((tm, tn), lambda i, j: (i, g0 + gstep + j)),
                  pl.BlockSpec((tm, tn), lambda i, j: (i, g0 + 2 * gstep + j))],
        out_specs=pl.BlockSpec((tm, tn), lambda i, j: (i, j)),
        compiler_params=_params("parallel", "arbitrary"),
        name="merge",
    )(ya, yb, yc, w_branch, w_branch, w_branch, h, h, h)


def _lower_bound_kernel(x_ref, o_ref):
    x = x_ref[...].astype(F32)
    e = jnp.exp(x - jnp.max(x, axis=0, keepdims=True))
    sm = e / jnp.sum(e, axis=0, keepdims=True)
    row = lax.broadcasted_iota(jnp.int32, x.shape, 0)
    acc = jnp.zeros_like(x)
    for i in range(1, x.shape[0]):
        acc = acc + jnp.where(row >= i, sm[i:i + 1, :], 0.0)
    o_ref[...] = acc


def _lower_bounds(lb_logits):
    return pl.pallas_call(
        _lower_bound_kernel,
        out_shape=jax.ShapeDtypeStruct(lb_logits.shape, F32),
        name="hgrn_lower_bound",
    )(lb_logits)


def _toeplitz_kernel(w_ref, mask_ref, o_ref):
    rows, cols = o_ref.shape[-2:]
    p = w_ref.shape[-1]
    x = jnp.broadcast_to(w_ref[0, 0], (rows, p))
    x = pltpu.roll(x, 0, 1, stride=1, stride_axis=0)
    o_ref[0, 0] = x[:, :cols] + mask_ref[0]


def _rel_bias_blocks(rel_bias, rows, cols, deltas, masks):
    p = -(-(rows + cols - 1) // LANE) * LANE
    k = np.arange(p)
    diag = np.where(k < cols, -k, p - k)
    idx = np.stack([np.clip(diag + dl, -MAX_REL, MAX_REL) + MAX_REL for dl in deltas])
    n_tab = rel_bias.shape[0] * rel_bias.shape[1]
    w = jnp.take(rel_bias.astype(F32).reshape(n_tab, -1), jnp.asarray(idx.reshape(-1)), axis=1)
    w = w.reshape(n_tab, len(deltas), 1, p)
    return pl.pallas_call(
        _toeplitz_kernel,
        out_shape=jax.ShapeDtypeStruct((n_tab, len(deltas), rows, cols), F32),
        grid=(n_tab, len(deltas)),
        in_specs=[pl.BlockSpec((1, 1, 1, p), lambda i, j: (i, j, 0, 0)),
                  pl.BlockSpec((1, rows, cols), lambda i, j: (j, 0, 0))],
        out_specs=pl.BlockSpec((1, 1, rows, cols), lambda i, j: (i, j, 0, 0)),
        compiler_params=_params("parallel", "parallel"),
        name="rel_bias_blocks",
    )(w, jnp.asarray(masks, F32))


def _band_mask(q_pos, k_pos):
    qc, kc = q_pos // CHUNK, k_pos // CHUNK
    valid = (kc[None, :] <= qc[:, None]) & (kc[None, :] >= qc[:, None] - N_LEFT_CHUNKS)
    return np.where(valid, 0.0, NEG)


def _prompt_bias(rel_bias):
    r = np.arange(ATT_TQ)
    deltas = [(ATT_NKB - 1 - j) * ATT_TQ for j in range(ATT_NKB)]
    base = BAND
    masks = np.stack([_band_mask(base + r, base - dl + r) for dl in deltas])
    return _rel_bias_blocks(rel_bias, ATT_TQ, ATT_TQ, deltas, masks)


def _sample_bias(rel_bias, t, ca):
    q_pos = PAST_LEN + np.arange(t)
    k_pos = np.concatenate([PAST_LEN - ca + np.arange(ca), q_pos])
    mask = _band_mask(q_pos, k_pos) + np.where(k_pos >= 0, 0.0, NEG)[None, :]
    b = _rel_bias_blocks(rel_bias, t, ca + t, [ca], mask[None])[:, 0]
    return b[:, :, :ca], b[:, :, ca:]


def _softmax_pv(scores, values):
    m = jnp.max(scores[0], axis=-1, keepdims=True)
    for s in scores[1:]:
        m = jnp.maximum(m, jnp.max(s, axis=-1, keepdims=True))
    l = None
    o = None
    for s, v in zip(scores, values):
        p = jnp.exp(s - m)
        ps = jnp.sum(p, axis=-1, keepdims=True)
        pv = jnp.dot(p.astype(BF16), v, preferred_element_type=F32)
        l = ps if l is None else l + ps
        o = pv if o is None else o + pv
    return o / l


def _attn_prompt_kernel(q_ref, *refs, n_heads, dh, nkb, scale):
    k_refs = refs[:nkb]
    v_refs = refs[nkb:2 * nkb]
    z_ref, bias_ref, o_ref = refs[2 * nkb:]
    t = pl.program_id(1)
    for hd in range(n_heads):
        sl = slice(hd * dh, (hd + 1) * dh)
        q = q_ref[0, :, sl]
        scores = []
        for j in range(nkb):
            s = lax.dot_general(q, k_refs[j][0, :, sl], NT_DIMS, preferred_element_type=F32)
            s = s * scale + bias_ref[hd, j]
            if j < nkb - 1:
                s = jnp.where(t >= nkb - 1 - j, s, NEG)
            scores.append(s)
        o = _softmax_pv(scores, [v_refs[j][0, :, sl] for j in range(nkb)])
        z = z_ref[0, :, sl].astype(F32)
        o_ref[0, :, sl] = (o * _silu(z)).astype(o_ref.dtype)


def _attn_prompt(h3, bias, layer, d_a):
    bsz, t, _ = h3.shape
    n_heads = d_a // DH_A
    nkb = ATT_NKB
    assert t % ATT_TQ == 0 and ATT_TQ % CHUNK == 0 and BAND % ATT_TQ == 0

    def kv_spec(col, j):
        return pl.BlockSpec((1, ATT_TQ, d_a),
                            lambda b, i: (b, jnp.maximum(i - (nkb - 1 - j), 0), col))

    in_specs = ([pl.BlockSpec((1, ATT_TQ, d_a), lambda b, i: (b, i, 0))]
                + [kv_spec(1, j) for j in range(nkb)]
                + [kv_spec(2, j) for j in range(nkb)]
                + [pl.BlockSpec((1, ATT_TQ, d_a), lambda b, i: (b, i, 3)),
                   pl.BlockSpec((n_heads,) + bias.shape[1:], lambda b, i: (layer, 0, 0, 0))])
    return pl.pallas_call(
        functools.partial(_attn_prompt_kernel, n_heads=n_heads, dh=DH_A, nkb=nkb, scale=DH_A ** -0.5),
        out_shape=jax.ShapeDtypeStruct((bsz, t, d_a), BF16),
        grid=(bsz, t // ATT_TQ),
        in_specs=in_specs,
        out_specs=pl.BlockSpec((1, ATT_TQ, d_a), lambda b, i: (b, i, 0)),
        compiler_params=_params("parallel", "arbitrary"),
        name="attn_prompt",
    )(h3, *([h3] * (2 * nkb)), h3, bias)


def _attn_sample_kernel(q_ref, k_ref, v_ref, z_ref, kc_ref, vc_ref, bc_ref, bn_ref, o_ref, *, n_heads, dh, scale):
    for hd in range(n_heads):
        sl = slice(hd * dh, (hd + 1) * dh)
        q = q_ref[0, :, sl]
        s_c = lax.dot_general(q, kc_ref[0, :, sl].astype(BF16), NT_DIMS, preferred_element_type=F32)
        s_n = lax.dot_general(q, k_ref[0, :, sl], NT_DIMS, preferred_element_type=F32)
        scores = [s_c * scale + bc_ref[hd], s_n * scale + bn_ref[hd]]
        o = _softmax_pv(scores, [vc_ref[0, :, sl].astype(BF16), v_ref[0, :, sl]])
        z = z_ref[0, :, sl].astype(F32)
        o_ref[0, :, sl] = (o * _silu(z)).astype(o_ref.dtype)


def _attn_sample(h3, k_cache, v_cache, bias_c, bias_n, layer, d_a):
    bsz, t, _ = h3.shape
    ca = k_cache.shape[2]
    n_heads = d_a // DH_A
    row = lambda col: pl.BlockSpec((1, t, d_a), lambda b: (b, 0, col))
    cache = pl.BlockSpec((None, 1, ca, d_a), lambda b: (layer, b, 0, 0))
    bias = lambda a: pl.BlockSpec((n_heads,) + a.shape[1:], lambda b: (layer, 0, 0))
    return pl.pallas_call(
        functools.partial(_attn_sample_kernel, n_heads=n_heads, dh=DH_A, scale=DH_A ** -0.5),
        out_shape=jax.ShapeDtypeStruct((bsz, t, d_a), BF16),
        grid=(bsz,),
        in_specs=[row(0), row(1), row(2), row(3), cache, cache, bias(bias_c), bias(bias_n)],
        out_specs=pl.BlockSpec((1, t, d_a), lambda b: (b, 0, 0)),
        compiler_params=_params("parallel"),
        name="attn_sample",
    )(h3, h3, h3, h3, k_cache, v_cache, bias_c, bias_n)


def _mem_attn_kernel(q_ref, z_ref, mk_ref, mv_ref, o_ref, *, n_heads, dh, scale):
    for hd in range(n_heads):
        sl = slice(hd * dh, (hd + 1) * dh)
        s = lax.dot_general(q_ref[0, :, sl], mk_ref[0, :, sl].astype(BF16), NT_DIMS,
                            preferred_element_type=F32) * scale
        o = _softmax_pv([s], [mv_ref[0, :, sl].astype(BF16)])
        z = z_ref[0, :, sl].astype(F32)
        o_ref[0, :, sl] = (o * _silu(z)).astype(o_ref.dtype)


def _mem_attn(h3, mk, mv, mk_spec, mv_spec, d_c, q_col0, tq):
    bsz, t, _ = h3.shape
    tq = min(tq, t)
    assert q_col0 % d_c == 0
    qb = q_col0 // d_c
    dh = d_c // H_C
    return pl.pallas_call(
        functools.partial(_mem_attn_kernel, n_heads=H_C, dh=dh, scale=dh ** -0.5),
        out_shape=jax.ShapeDtypeStruct((bsz, t, d_c), BF16),
        grid=(bsz, t // tq),
        in_specs=[pl.BlockSpec((1, tq, d_c), lambda b, i: (b, i, qb)),
                  pl.BlockSpec((1, tq, d_c), lambda b, i: (b, i, qb + 1)),
                  mk_spec, mv_spec],
        out_specs=pl.BlockSpec((1, tq, d_c), lambda b, i: (b, i, 0)),
        compiler_params=_params("parallel", "arbitrary"),
        name="mem_attn",
    )(h3, h3, mk, mv)


def _seg_cumsum(x, row, group):
    d = 1
    while d < group:
        x = x + jnp.where((row & (group - 1)) >= d, pltpu.roll(x, d, 0), 0.0)
        d *= 2
    return x


def _hgrn_masks():
    t = np.arange(HG_L)[:, None]
    s = np.arange(HG_L)[None, :]
    m16 = [(t // 16 == i) & (s < 16 * i) for i in range(1, 4)]
    m4 = [(t // 16 == s // 16) & ((t % 16) // 4 == i) & (s % 16 < 4 * i) for i in range(1, 4)]
    base = (t // 4 == s // 4) & (s <= t)
    m1 = np.concatenate(m16, axis=1).astype(np.float32)
    m2 = np.concatenate(m4 + [base], axis=1).astype(np.float32)
    return jnp.asarray(m1), jnp.asarray(m2)


def _hgrn_block(q, g, k, v, st, m1, m2, cs_ref):
    L = HG_L
    row = lax.broadcasted_iota(jnp.int32, (L, LANE), 0)
    c4 = _seg_cumsum(g, row, 4)
    c16 = _seg_cumsum(g, row, 16)
    cs_ref[0] = c16
    b = c16
    for j in range(1, L // 16):
        b = b + jnp.where(row >= 16 * j, cs_ref[0, 16 * j - 1:16 * j, :], 0.0)
    cs_ref[1] = b

    q4 = (q * jnp.exp(c4)).astype(BF16)
    q16 = (q * jnp.exp(c16)).astype(BF16)
    q64 = (q * jnp.exp(b)).astype(BF16)

    k16 = []
    for i in range(1, L // 16):
        ref_row = cs_ref[1, 16 * i - 1:16 * i, :]
        k16.append(k * jnp.exp(jnp.minimum(ref_row - b, 0.0)))
    k4 = []
    for i in range(1, 4):
        ref_rows = jnp.concatenate(
            [jnp.broadcast_to(cs_ref[0, 16 * j + 4 * i - 1:16 * j + 4 * i, :], (16, LANE))
             for j in range(L // 16)], axis=0)
        k4.append(k * jnp.exp(jnp.minimum(ref_rows - c16, 0.0)))
    k_base = k * jnp.exp(-c4)
    ks1 = jnp.concatenate(k16, axis=0).astype(BF16)
    ks2 = jnp.concatenate(k4 + [k_base], axis=0).astype(BF16)

    a1 = lax.dot_general(q16, ks1, NT_DIMS, preferred_element_type=F32) * m1
    a2 = lax.dot_general(q4, ks2, NT_DIMS, preferred_element_type=F32) * m2
    vb = v.astype(BF16)
    o = (jnp.dot(a1.astype(BF16), jnp.concatenate([vb] * 3, axis=0), preferred_element_type=F32)
         + jnp.dot(a2.astype(BF16), jnp.concatenate([vb] * 4, axis=0), preferred_element_type=F32)
         + lax.dot_general(q64, st.astype(BF16), NT_DIMS, preferred_element_type=F32))

    b_last = cs_ref[1, L - 1:L, :]
    kd = (k * jnp.exp(b_last - b)).astype(BF16)
    st_new = st * jnp.exp(b_last) + lax.dot_general(vb, kd, TN_DIMS, preferred_element_type=F32)
    return o, st_new


def _hgrn_kernel(q_ref, f_ref, i_ref, z_ref, lb_ref, gn_ref, s0_ref, m1_ref, m2_ref,
                 y_ref, s_out_ref, st_ref, cs_ref, *, rows, n_sub, unroll):
    tb = pl.program_id(2)

    @pl.when(tb == 0)
    def _():
        for p in range(HG_HP):
            st_ref[p] = s0_ref[0, p].astype(F32).T

    m1 = m1_ref[...]
    m2 = m2_ref[...]

    def pad(x):
        if rows == HG_L:
            return x
        return jnp.concatenate([x, jnp.zeros((HG_L - rows, LANE), F32)], axis=0)

    def one_block(r0, p, st, slot):
        sl = slice(p * LANE, (p + 1) * LANE)
        qr = q_ref[0, pl.ds(r0, rows), sl].astype(F32)
        fr = f_ref[0, pl.ds(r0, rows), sl].astype(F32)
        vr = i_ref[0, pl.ds(r0, rows), sl].astype(F32)
        zr = z_ref[0, pl.ds(r0, rows), sl].astype(F32)
        kf = (1.0 - lb_ref[p]) * jax.nn.sigmoid(-fr)
        g = jnp.log1p(-kf)
        o, st = _hgrn_block(pad(_silu(qr)), pad(g), pad(kf), pad(vr), st, m1, m2, cs_ref.at[slot])
        o = o[:rows]
        on = o * lax.rsqrt(jnp.mean(o * o, axis=-1, keepdims=True) + EPS) * gn_ref[p]
        y_ref[0, pl.ds(r0, rows), sl] = (on * _silu(zr)).astype(y_ref.dtype)
        return st

    def body(it, sts):
        sts = list(sts)
        for u in range(unroll):
            r0 = pl.multiple_of((it * unroll + u) * rows, rows)
            for p in range(HG_HP):
                sts[p] = one_block(r0, p, sts[p], u * HG_HP + p)
        return tuple(sts)

    sts = lax.fori_loop(0, n_sub // unroll, body, tuple(st_ref[p] for p in range(HG_HP)))
    for p in range(HG_HP):
        st_ref[p] = sts[p]

    @pl.when(tb == pl.num_programs(2) - 1)
    def _():
        for p in range(HG_HP):
            s_out_ref[0, p] = sts[p].T


def _hgrn(h3, lb, gain, s0, layer, masks, col0, d_b):
    bsz, t, _ = h3.shape
    n_heads = d_b // DK_B
    assert col0 % (HG_HP * LANE) == 0 and n_heads % HG_HP == 0 and DK_B == LANE
    assert s0.shape[1:] == (bsz, n_heads, DK_B, LANE)
    if t % HG_L == 0:
        rows, tb = HG_L, min(HG_TB, t)
    else:
        assert t < HG_L and t % 16 == 0
        rows, tb = t, t
    n_sub = tb // rows
    unroll = HG_UNROLL if n_sub % HG_UNROLL == 0 else 1
    assert t % tb == 0
    w = HG_HP * LANE
    c0 = col0 // w
    col = lambda seg: pl.BlockSpec((1, tb, w), lambda b, hp, i: (b, i, c0 + seg * (n_heads // HG_HP) + hp))
    per_head = pl.BlockSpec((HG_HP, 1, LANE), lambda b, hp, i: (hp, 0, 0))
    m1, m2 = masks
    y, s_new = pl.pallas_call(
        functools.partial(_hgrn_kernel, rows=rows, n_sub=n_sub, unroll=unroll),
        out_shape=(jax.ShapeDtypeStruct((bsz, t, d_b), BF16),
                   jax.ShapeDtypeStruct(s0.shape[1:], F32)),
        grid=(bsz, n_heads // HG_HP, t // tb),
        in_specs=[col(0), col(1), col(2), col(3), per_head, per_head,
                  pl.BlockSpec((None, 1, HG_HP, DK_B, LANE), lambda b, hp, i: (layer, b, hp, 0, 0)),
                  pl.BlockSpec(m1.shape, lambda b, hp, i: (0, 0)),
                  pl.BlockSpec(m2.shape, lambda b, hp, i: (0, 0))],
        out_specs=(pl.BlockSpec((1, tb, w), lambda b, hp, i: (b, i, hp)),
                   pl.BlockSpec((1, HG_HP, DK_B, LANE), lambda b, hp, i: (b, hp, 0, 0))),
        scratch_shapes=[pltpu.VMEM((HG_HP, LANE, DK_B), F32),
                        pltpu.VMEM((HG_UNROLL * HG_HP, 2, HG_L, LANE), F32)],
        compiler_params=_params("parallel", "parallel", "arbitrary"),
        name="hgrn2",
    )(h3, h3, h3, h3, lb.reshape(n_heads, 1, DK_B), gain.reshape(n_heads, 1, LANE).astype(F32),
      s0, m1, m2)
    return y, s_new


def _layer(x, layer, attend, s0, s0_layer, mem_args, norm_g, w_in, lb, hgrn_g, w_branch, w_out, masks, dims):
    d_a, d_b, d_c = dims
    bsz, t, d = x.shape
    m = bsz * t
    x2 = x.reshape(m, d)
    xn = _rmsnorm(x2, norm_g, BF16, 256)
    h = _matmul(xn, w_in, layer, BF16, 1024, 1024)
    h3 = h.reshape(bsz, t, -1)
    ya = attend(h3)
    yb, s_new = _hgrn(h3, lb, hgrn_g, s0, s0_layer, masks, 4 * d_a, d_b)
    yc = _mem_attn(h3, *mem_args, d_c, 4 * d_a + 4 * d_b, 512)
    merged = _merge(ya.reshape(m, d_a), yb.reshape(m, d_b), yc.reshape(m, d_c), h, w_branch, layer,
                    4 * d_a + 4 * d_b + 2 * d_c, 1024, 512)
    x_new = _matmul(merged, w_out, layer, F32, 1024, 512, res=x2).reshape(bsz, t, d)
    ka = h3[:, :, d_a:2 * d_a]
    va = h3[:, :, 2 * d_a:3 * d_a]
    return x_new, ka, va, s_new


def kernel(x_prompt, x_sample, mem_prompt, cache_attn_k, cache_attn_v, state_hgrn, cache_mem_k, cache_mem_v, norm_gain, w_in, rel_bias, lb_logits, hgrn_norm_gain, mem_norm_gain, w_mem_kv, w_branch, w_out, final_norm_gain):
    depth = w_in.shape[0]
    bp, tp, d = x_prompt.shape
    bs, ts, _ = x_sample.shape
    h_a = rel_bias.shape[1]
    d_a = h_a * DH_A
    d_b = lb_logits.shape[1]
    d_c = w_mem_kv.shape[2] // 2
    h_b = d_b // DK_B
    n_mem = mem_prompt.shape[1]
    dims = (d_a, d_b, d_c)
    ca_s = cache_attn_k.shape[2]
    ca_p = min(BAND, tp)

    w_in_b = w_in.astype(BF16)
    w_branch_b = w_branch.astype(BF16)
    w_out_b = w_out.astype(BF16)
    w_mem_b = w_mem_kv.astype(BF16)
    kc_s = cache_attn_k.reshape(depth, bs, ca_s, d_a)
    vc_s = cache_attn_v.reshape(depth, bs, ca_s, d_a)
    mk_s = cache_mem_k.reshape(depth, bs, n_mem, d_c)
    mv_s = cache_mem_v.reshape(depth, bs, n_mem, d_c)

    lb_all = _lower_bounds(lb_logits)
    masks = _hgrn_masks()
    bias_p = _prompt_bias(rel_bias)
    bias_c, bias_n = _sample_bias(rel_bias, ts, ca_s)
    s0_p = jnp.zeros((1, bp, h_b, DK_B, d_b // h_b), F32)
    mem2 = mem_prompt.reshape(bp * n_mem, d)

    xp, xs = x_prompt, x_sample
    kp_l, vp_l, sp_l, mkp_l, mvp_l, ks_l, vs_l, ss_l = [], [], [], [], [], [], [], []
    for l in range(depth):
        shared = (norm_gain[l], w_in_b, lb_all[l], hgrn_norm_gain[l], w_branch_b, w_out_b, masks, dims)

        memn = _rmsnorm(mem2, mem_norm_gain[l], BF16, 256)
        mkv = _matmul(memn, w_mem_b, l, F32, 1024, 1024).reshape(bp, n_mem, 2 * d_c)
        mem_p = (mkv, mkv,
                 pl.BlockSpec((1, n_mem, d_c), lambda b, i: (b, 0, 0)),
                 pl.BlockSpec((1, n_mem, d_c), lambda b, i: (b, 0, 1)))
        att_p = lambda h3: _attn_prompt(h3, bias_p, l, d_a)
        xp, ka, va, s_fin = _layer(xp, l, att_p, s0_p, 0, mem_p, *shared)
        kp_l.append(ka[:, -ca_p:].astype(F32).reshape(bp, ca_p, h_a, DH_A))
        vp_l.append(va[:, -ca_p:].astype(F32).reshape(bp, ca_p, h_a, DH_A))
        sp_l.append(s_fin)
        mkp_l.append(mkv[:, :, :d_c].reshape(bp, n_mem, H_C, d_c // H_C))
        mvp_l.append(mkv[:, :, d_c:].reshape(bp, n_mem, H_C, d_c // H_C))

        mem_s = (mk_s, mv_s,
                 pl.BlockSpec((None, 1, n_mem, d_c), lambda b, i: (l, b, 0, 0)),
                 pl.BlockSpec((None, 1, n_mem, d_c), lambda b, i: (l, b, 0, 0)))
        att_s = lambda h3: _attn_sample(h3, kc_s, vc_s, bias_c, bias_n, l, d_a)
        xs, ka_s, va_s, s_new = _layer(xs, l, att_s, state_hgrn, l, mem_s, *shared)
        ks_l.append(ka_s.astype(F32).reshape(bs, ts, h_a, DH_A))
        vs_l.append(va_s.astype(F32).reshape(bs, ts, h_a, DH_A))
        ss_l.append(s_new)

    y_prompt = _rmsnorm(xp.reshape(bp * tp, d), final_norm_gain, F32, 256).reshape(bp, tp, d)
    y_sample = _rmsnorm(xs.reshape(bs * ts, d), final_norm_gain, F32, 256).reshape(bs, ts, d)
    return (y_prompt, y_sample,
            jnp.stack(kp_l), jnp.stack(vp_l), jnp.stack(sp_l), jnp.stack(mkp_l), jnp.stack(mvp_l),
            jnp.stack(ks_l), jnp.stack(vs_l), jnp.stack(ss_l))
```

```python
import functools

import numpy as np
import jax
import jax.numpy as jnp
from jax import lax
from jax.experimental import pallas as pl
from jax.experimental.pallas import tpu as pltpu

F32 = jnp.float32
BF16 = jnp.bfloat16

PAST_LEN = 4096
CHUNK = 64
N_LEFT_CHUNKS = 8
BAND = N_LEFT_CHUNKS * CHUNK
DH_A = 128
MAX_REL = 256
DK_B = 128
H_C = 4
EPS = 1e-6
NEG = -1e30
LOG2E = 1.4426950408889634

LANE = 128
VMEM_LIMIT_BYTES = 56 * 1024 * 1024

ATT_TQ = 256
ATT_NKB = BAND // ATT_TQ + 1
HG_L = 64
HG_TB = 512
HG_HP = 4
HG_UNROLL = 2

NT_DIMS = (((1,), (1,)), ((), ()))
TN_DIMS = (((0,), (0,)), ((), ()))


def _params(*sem, flags=None):
    return pltpu.CompilerParams(dimension_semantics=sem, vmem_limit_bytes=VMEM_LIMIT_BYTES, flags=flags)


def _silu(z):
    return z * jax.nn.sigmoid(z)


def _rmsnorm_kernel(x_ref, g_ref, o_ref):
    x = x_ref[...].astype(F32)
    y = x * lax.rsqrt(jnp.mean(x * x, axis=-1, keepdims=True) + EPS)
    o_ref[...] = (y * g_ref[...]).astype(o_ref.dtype)


def _rmsnorm(x, g, out_dtype, tm):
    m, d = x.shape
    tm = min(tm, m)
    return pl.pallas_call(
        _rmsnorm_kernel,
        out_shape=jax.ShapeDtypeStruct((m, d), out_dtype),
        grid=(m // tm,),
        in_specs=[pl.BlockSpec((tm, d), lambda i: (i, 0)),
                  pl.BlockSpec((1, d), lambda i: (0, 0))],
        out_specs=pl.BlockSpec((tm, d), lambda i: (i, 0)),
        compiler_params=_params("parallel"),
        name="rmsnorm",
    )(x, g.reshape(1, d).astype(F32))


def _mm_kernel(a_ref, b_ref, o_ref):
    o_ref[...] = jnp.dot(a_ref[...], b_ref[...], preferred_element_type=F32).astype(o_ref.dtype)


def _mm_res_kernel(a_ref, b_ref, r_ref, o_ref):
    o_ref[...] = r_ref[...] + jnp.dot(a_ref[...], b_ref[...], preferred_element_type=F32)


def _matmul(a, w, layer, out_dtype, tm, tn, res=None):
    m, k = a.shape
    n = w.shape[2]
    tm, tn = min(tm, m), min(tn, n)
    in_specs = [pl.BlockSpec((tm, k), lambda i, j: (i, 0)),
                pl.BlockSpec((None, k, tn), lambda i, j: (layer, 0, j))]
    args = [a, w]
    kern = _mm_kernel
    if res is not None:
        in_specs.append(pl.BlockSpec((tm, tn), lambda i, j: (i, j)))
        args.append(res)
        kern = _mm_res_kernel
    return pl.pallas_call(
        kern,
        out_shape=jax.ShapeDtypeStruct((m, n), out_dtype),
        grid=(m // tm, n // tn),
        in_specs=in_specs,
        out_specs=pl.BlockSpec((tm, tn), lambda i, j: (i, j)),
        compiler_params=_params("parallel", "arbitrary"),
        name="matmul_res" if res is not None else "matmul",
    )(*args)


def _merge_kernel(ya_ref, yb_ref, yc_ref, wa_ref, wb_ref, wc_ref, ga_ref, gb_ref, gc_ref, o_ref):
    def part(y_ref, w_ref, g_ref):
        gate = jax.nn.sigmoid(g_ref[...].astype(F32))
        return gate * jnp.dot(y_ref[...], w_ref[...], preferred_element_type=F32)

    o_ref[...] = (part(ya_ref, wa_ref, ga_ref) + part(yb_ref, wb_ref, gb_ref)
                  + part(yc_ref, wc_ref, gc_ref)).astype(o_ref.dtype)


def _merge(ya, yb, yc, h, w_branch, layer, gate_col0, tm, tn):
    m, d_a = ya.shape
    d_b, d_c = yb.shape[1], yc.shape[1]
    d = w_branch.shape[2]
    tm = min(tm, m)
    assert d_a == d_b and (d_a + d_b) % d_c == 0 and gate_col0 % tn == 0 and d % tn == 0
    g0, gstep = gate_col0 // tn, d // tn
    return pl.pallas_call(
        _merge_kernel,
        out_shape=jax.ShapeDtypeStruct((m, d), BF16),
        grid=(m // tm, d // tn),
        in_specs=[pl.BlockSpec((tm, d_a), lambda i, j: (i, 0)),
                  pl.BlockSpec((tm, d_b), lambda i, j: (i, 0)),
                  pl.BlockSpec((tm, d_c), lambda i, j: (i, 0)),
                  pl.BlockSpec((None, d_a, tn), lambda i, j: (layer, 0, j)),
                  pl.BlockSpec((None, d_b, tn), lambda i, j: (layer, 1, j)),
                  pl.BlockSpec((None, d_c, tn), lambda i, j: (layer, (d_a + d_b) // d_c, j)),
                  pl.BlockSpec((tm, tn), lambda i, j: (i, g0 + j)),
                  pl.BlockSpec((tm, tn), lambda i, j: (i, g0 + gstep + j)),
                  pl.BlockSpec((tm, tn), lambda i, j: (i, g0 + 2 * gstep + j))],
        out_specs=pl.BlockSpec((tm, tn), lambda i, j: (i, j)),
        compiler_params=_params("parallel", "arbitrary"),
        name="merge",
    )(ya, yb, yc, w_branch, w_branch, w_branch, h, h, h)


def _lower_bound_kernel(x_ref, o_ref):
    x = x_ref[...].astype(F32)
    e = jnp.exp(x - jnp.max(x, axis=0, keepdims=True))
    sm = e / jnp.sum(e, axis=0, keepdims=True)
    row = lax.broadcasted_iota(jnp.int32, x.shape, 0)
    acc = jnp.zeros_like(x)
    for i in range(1, x.shape[0]):
        acc = acc + jnp.where(row >= i, sm[i:i + 1, :], 0.0)
    o_ref[...] = acc


def _lower_bounds(lb_logits):
    return pl.pallas_call(
        _lower_bound_kernel,
        out_shape=jax.ShapeDtypeStruct(lb_logits.shape, F32),
        name="hgrn_lower_bound",
    )(lb_logits)


def _toeplitz_kernel(w_ref, mask_ref, o_ref, *, mult):
    rows, cols = o_ref.shape[-2:]
    p = w_ref.shape[-1]
    x = jnp.broadcast_to(w_ref[0, 0], (rows, p))
    x = pltpu.roll(x, 0, 1, stride=1, stride_axis=0)
    o_ref[0, 0] = x[:, :cols] * mult + mask_ref[0]


def _rel_bias_blocks(rel_bias, rows, cols, deltas, masks):
    p = -(-(rows + cols - 1) // LANE) * LANE
    k = np.arange(p)
    diag = np.where(k < cols, -k, p - k)
    idx = np.stack([np.clip(diag + dl, -MAX_REL, MAX_REL) + MAX_REL for dl in deltas])
    n_tab = rel_bias.shape[0] * rel_bias.shape[1]
    w = jnp.take(rel_bias.astype(F32).reshape(n_tab, -1), jnp.asarray(idx.reshape(-1)), axis=1)
    w = w.reshape(n_tab, len(deltas), 1, p)
    return pl.pallas_call(
        functools.partial(_toeplitz_kernel, mult=DH_A ** 0.5),
        out_shape=jax.ShapeDtypeStruct((n_tab, len(deltas), rows, cols), F32),
        grid=(n_tab, len(deltas)),
        in_specs=[pl.BlockSpec((1, 1, 1, p), lambda i, j: (i, j, 0, 0)),
                  pl.BlockSpec((1, rows, cols), lambda i, j: (j, 0, 0))],
        out_specs=pl.BlockSpec((1, 1, rows, cols), lambda i, j: (i, j, 0, 0)),
        compiler_params=_params("parallel", "parallel"),
        name="rel_bias_blocks",
    )(w, jnp.asarray(masks, F32))


def _band_mask(q_pos, k_pos):
    qc, kc = q_pos // CHUNK, k_pos // CHUNK
    valid = (kc[None, :] <= qc[:, None]) & (kc[None, :] >= qc[:, None] - N_LEFT_CHUNKS)
    return np.where(valid, 0.0, NEG)


def _prompt_bias(rel_bias):
    r = np.arange(ATT_TQ)
    deltas = [(ATT_NKB - 1 - j) * ATT_TQ for j in range(ATT_NKB)]
    base = BAND
    masks = np.stack([_band_mask(base + r, base - dl + r) for dl in deltas])
    return _rel_bias_blocks(rel_bias, ATT_TQ, ATT_TQ, deltas, masks)


def _sample_bias(rel_bias, t, ca):
    q_pos = PAST_LEN + np.arange(t)
    k_pos = np.concatenate([PAST_LEN - ca + np.arange(ca), q_pos])
    mask = _band_mask(q_pos, k_pos) + np.where(k_pos >= 0, 0.0, NEG)[None, :]
    b = _rel_bias_blocks(rel_bias, t, ca + t, [ca], mask[None])[:, 0]
    return b[:, :, :ca], b[:, :, ca:]


def _softmax_pv(scores, values, c, ones_col):
    if all(s.shape == scores[0].shape for s in scores):
        mx = scores[0]
        for s in scores[1:]:
            mx = jnp.maximum(mx, s)
        m = jnp.max(mx, axis=-1, keepdims=True)
    else:
        m = jnp.max(scores[0], axis=-1, keepdims=True)
        for s in scores[1:]:
            m = jnp.maximum(m, jnp.max(s, axis=-1, keepdims=True))
    dh = values[0].shape[1]
    l = None
    o = None
    for s, v in zip(scores, values):
        p = jnp.exp2((s - m) * c).astype(BF16)
        if ones_col:
            v = jnp.concatenate([v, jnp.ones_like(v)], axis=1)
        else:
            ps = jnp.sum(p.astype(F32), axis=-1, keepdims=True)
            l = ps if l is None else l + ps
        pv = jnp.dot(p, v, preferred_element_type=F32)
        o = pv if o is None else o + pv
    if ones_col:
        return o[:, :dh] / o[:, dh:]
    return o / l


def _attn_prompt_kernel(q_ref, *refs, n_heads, dh, nkb, scale):
    k_refs = refs[:nkb]
    v_refs = refs[nkb:2 * nkb]
    z_ref, bias_ref, o_ref = refs[2 * nkb:]
    t = pl.program_id(1)
    for hd in range(n_heads):
        sl = slice(hd * dh, (hd + 1) * dh)
        q = q_ref[0, :, sl]
        scores = []
        for j in range(nkb):
            s = lax.dot_general(q, k_refs[j][0, :, sl], NT_DIMS, preferred_element_type=F32)
            s = s + bias_ref[hd, j]
            if j < nkb - 1:
                s = jnp.where(t >= nkb - 1 - j, s, NEG)
            scores.append(s)
        o = _softmax_pv(scores, [v_refs[j][0, :, sl] for j in range(nkb)], scale * LOG2E, True)
        z = z_ref[0, :, sl].astype(F32)
        o_ref[0, :, sl] = (o * _silu(z)).astype(o_ref.dtype)


def _attn_prompt(h3, bias, layer, d_a):
    bsz, t, _ = h3.shape
    n_heads = d_a // DH_A
    nkb = ATT_NKB
    assert t % ATT_TQ == 0 and ATT_TQ % CHUNK == 0 and BAND % ATT_TQ == 0

    def kv_spec(col, j):
        return pl.BlockSpec((1, ATT_TQ, d_a),
                            lambda b, i: (b, jnp.maximum(i - (nkb - 1 - j), 0), col))

    in_specs = ([pl.BlockSpec((1, ATT_TQ, d_a), lambda b, i: (b, i, 0))]
                + [kv_spec(1, j) for j in range(nkb)]
                + [kv_spec(2, j) for j in range(nkb)]
                + [pl.BlockSpec((1, ATT_TQ, d_a), lambda b, i: (b, i, 3)),
                   pl.BlockSpec((n_heads,) + bias.shape[1:], lambda b, i: (layer, 0, 0, 0))])
    return pl.pallas_call(
        functools.partial(_attn_prompt_kernel, n_heads=n_heads, dh=DH_A, nkb=nkb, scale=DH_A ** -0.5),
        out_shape=jax.ShapeDtypeStruct((bsz, t, d_a), BF16),
        grid=(bsz, t // ATT_TQ),
        in_specs=in_specs,
        out_specs=pl.BlockSpec((1, ATT_TQ, d_a), lambda b, i: (b, i, 0)),
        compiler_params=_params("parallel", "arbitrary"),
        name="attn_prompt",
    )(h3, *([h3] * (2 * nkb)), h3, bias)


def _attn_sample_kernel(q_ref, k_ref, v_ref, z_ref, kc_ref, vc_ref, bc_ref, bn_ref, o_ref, *, n_heads, dh, scale):
    for hd in range(n_heads):
        sl = slice(hd * dh, (hd + 1) * dh)
        q = q_ref[0, :, sl]
        s_c = lax.dot_general(q, kc_ref[0, :, hd, :].astype(BF16), NT_DIMS, preferred_element_type=F32)
        s_n = lax.dot_general(q, k_ref[0, :, sl], NT_DIMS, preferred_element_type=F32)
        scores = [s_c + bc_ref[hd], s_n + bn_ref[hd]]
        o = _softmax_pv(scores, [vc_ref[0, :, hd, :].astype(BF16), v_ref[0, :, sl]], scale * LOG2E, True)
        z = z_ref[0, :, sl].astype(F32)
        o_ref[0, :, sl] = (o * _silu(z)).astype(o_ref.dtype)


def _attn_sample(h3, k_cache, v_cache, bias_c, bias_n, layer, d_a):
    bsz, t, _ = h3.shape
    ca = k_cache.shape[2]
    n_heads = d_a // DH_A
    assert k_cache.shape[3:] == (n_heads, DH_A)
    row = lambda col: pl.BlockSpec((1, t, d_a), lambda b: (b, 0, col))
    cache = pl.BlockSpec((None, 1, ca, n_heads, DH_A), lambda b: (layer, b, 0, 0, 0))
    bias = lambda a: pl.BlockSpec((n_heads,) + a.shape[1:], lambda b: (layer, 0, 0))
    return pl.pallas_call(
        functools.partial(_attn_sample_kernel, n_heads=n_heads, dh=DH_A, scale=DH_A ** -0.5),
        out_shape=jax.ShapeDtypeStruct((bsz, t, d_a), BF16),
        grid=(bsz,),
        in_specs=[row(0), row(1), row(2), row(3), cache, cache, bias(bias_c), bias(bias_n)],
        out_specs=pl.BlockSpec((1, t, d_a), lambda b: (b, 0, 0)),
        compiler_params=_params("parallel"),
        name="attn_sample",
    )(h3, h3, h3, h3, k_cache, v_cache, bias_c, bias_n)


def _mem_attn_kernel(q_ref, z_ref, mk_ref, mv_ref, o_ref, *, n_heads, dh, scale, head_axis):
    for hd in range(n_heads):
        sl = slice(hd * dh, (hd + 1) * dh)
        mk = mk_ref[0, :, hd, :] if head_axis else mk_ref[0, :, sl]
        mv = mv_ref[0, :, hd, :] if head_axis else mv_ref[0, :, sl]
        s = lax.dot_general(q_ref[0, :, sl], mk.astype(BF16), NT_DIMS, preferred_element_type=F32)
        o = _softmax_pv([s], [mv.astype(BF16)], scale * LOG2E, False)
        z = z_ref[0, :, sl].astype(F32)
        o_ref[0, :, sl] = (o * _silu(z)).astype(o_ref.dtype)


def _mem_attn(h3, mk, mv, mk_spec, mv_spec, head_axis, d_c, q_col0, tq):
    bsz, t, _ = h3.shape
    tq = min(tq, t)
    assert q_col0 % d_c == 0
    qb = q_col0 // d_c
    dh = d_c // H_C
    return pl.pallas_call(
        functools.partial(_mem_attn_kernel, n_heads=H_C, dh=dh, scale=dh ** -0.5, head_axis=head_axis),
        out_shape=jax.ShapeDtypeStruct((bsz, t, d_c), BF16),
        grid=(bsz, t // tq),
        in_specs=[pl.BlockSpec((1, tq, d_c), lambda b, i: (b, i, qb)),
                  pl.BlockSpec((1, tq, d_c), lambda b, i: (b, i, qb + 1)),
                  mk_spec, mv_spec],
        out_specs=pl.BlockSpec((1, tq, d_c), lambda b, i: (b, i, 0)),
        compiler_params=_params("parallel", "arbitrary"),
        name="mem_attn",
    )(h3, h3, mk, mv)


def _seg_cumsum(x, row, group):
    d = 1
    while d < group:
        x = x + jnp.where((row & (group - 1)) >= d, pltpu.roll(x, d, 0), 0.0)
        d *= 2
    return x


def _hgrn_masks():
    t = np.arange(HG_L)[:, None]
    s = np.arange(HG_L)[None, :]
    m16 = [(t // 16 == i) & (s < 16 * i) for i in range(1, 4)]
    m4 = [(t // 16 == s // 16) & ((t % 16) // 4 == i) & (s % 16 < 4 * i) for i in range(1, 4)]
    base = (t // 4 == s // 4) & (s <= t)
    m1 = np.concatenate(m16, axis=1).astype(np.float32)
    m2 = np.concatenate(m4 + [base], axis=1).astype(np.float32)
    return jnp.asarray(m1), jnp.asarray(m2)


def _bcast_rows(ref, offsets, reps):
    return jnp.concatenate(
        [jnp.broadcast_to(ref[o:o + 1, :], (reps, LANE)) for o in offsets], axis=0)


def _hgrn_blocks(q, g, k, v, sts, chain_head, m1, m2, cs_ref):
    L = HG_L
    n = q.shape[0] // L
    starts = [c * L for c in range(n)]
    row = lax.broadcasted_iota(jnp.int32, q.shape, 0)
    r64 = row & (L - 1)
    c4 = _seg_cumsum(g, row, 4)
    c16 = _seg_cumsum(g, row, 16)
    c16_ref, b_ref = cs_ref.at[0], cs_ref.at[1]
    c16_ref[...] = c16
    b = c16
    for j in range(1, L // 16):
        b = b + jnp.where(r64 >= 16 * j, _bcast_rows(c16_ref, [s + 16 * j - 1 for s in starts], L), 0.0)
    b_ref[...] = b

    q4 = (q * jnp.exp2(c4)).astype(BF16)
    q16 = (q * jnp.exp2(c16)).astype(BF16)
    q64 = (q * jnp.exp2(b)).astype(BF16)

    k16 = []
    for i in range(1, L // 16):
        ref_rows = _bcast_rows(b_ref, [s + 16 * i - 1 for s in starts], L)
        k16.append((k * jnp.exp2(jnp.minimum(ref_rows - b, 0.0))).astype(BF16))
    k4 = []
    for i in range(1, 4):
        ref_rows = _bcast_rows(c16_ref, [s + 16 * j + 4 * i - 1 for s in starts for j in range(L // 16)], 16)
        k4.append((k * jnp.exp2(jnp.minimum(ref_rows - c16, 0.0))).astype(BF16))
    k4.append((k * jnp.exp2(-c4)).astype(BF16))
    kd = (k * jnp.exp2(_bcast_rows(b_ref, [s + L - 1 for s in starts], L) - b)).astype(BF16)
    vb = v.astype(BF16)

    scores = []
    for s in starts:
        sl = slice(s, s + L)
        a1 = lax.dot_general(q16[sl], jnp.concatenate([x[sl] for x in k16], axis=0), NT_DIMS,
                             preferred_element_type=F32) * m1
        a2 = lax.dot_general(q4[sl], jnp.concatenate([x[sl] for x in k4], axis=0), NT_DIMS,
                             preferred_element_type=F32) * m2
        scores.append((a1.astype(BF16), a2.astype(BF16)))
    sts = list(sts)
    outs = []
    for c, s in enumerate(starts):
        sl = slice(s, s + L)
        a1, a2 = scores[c]
        st = sts[chain_head[c]]
        vc = vb[sl]
        outs.append(
            jnp.dot(a1, jnp.concatenate([vc] * 3, axis=0), preferred_element_type=F32)
            + jnp.dot(a2, jnp.concatenate([vc] * 4, axis=0), preferred_element_type=F32)
            + lax.dot_general(q64[sl], st.astype(BF16), NT_DIMS, preferred_element_type=F32))
        sts[chain_head[c]] = (st * jnp.exp2(b_ref[s + L - 1:s + L, :])
                              + lax.dot_general(vc, kd[sl], TN_DIMS, preferred_element_type=F32))
    return jnp.concatenate(outs, axis=0), sts


def _hgrn_kernel(q_ref, f_ref, i_ref, z_ref, lb_ref, gn_ref, s0_ref, m1_ref, m2_ref,
                 y_ref, s_out_ref, st_ref, cs_ref, *, rows, n_sub, unroll):
    tb = pl.program_id(2)

    @pl.when(tb == 0)
    def _():
        for p in range(HG_HP):
            st_ref[p] = s0_ref[0, p].astype(F32).T

    m1 = m1_ref[...]
    m2 = m2_ref[...]
    chains = [(u, p) for u in range(unroll) for p in range(HG_HP)]
    chain_head = [p for _, p in chains]
    lb = jnp.concatenate([jnp.broadcast_to(lb_ref[p], (HG_L, LANE)) for _, p in chains], axis=0)
    gn = jnp.concatenate([jnp.broadcast_to(gn_ref[p], (HG_L, LANE)) for _, p in chains], axis=0)

    def stack(ref, r0s):
        parts = []
        for u, p in chains:
            x = ref[0, pl.ds(r0s[u], rows), p * LANE:(p + 1) * LANE].astype(F32)
            if rows < HG_L:
                x = jnp.concatenate([x, jnp.zeros((HG_L - rows, LANE), F32)], axis=0)
            parts.append(x)
        return jnp.concatenate(parts, axis=0)

    def body(it, sts):
        r0s = [pl.multiple_of((it * unroll + u) * rows, rows) for u in range(unroll)]
        qr, fr, vr, zr = (stack(ref, r0s) for ref in (q_ref, f_ref, i_ref, z_ref))
        kf = (1.0 - lb) * jax.nn.sigmoid(-fr)
        g = jnp.log1p(-kf) * LOG2E
        if rows < HG_L:
            live = (lax.broadcasted_iota(jnp.int32, fr.shape, 0) & (HG_L - 1)) < rows
            kf = jnp.where(live, kf, 0.0)
            g = jnp.where(live, g, 0.0)
        o, sts = _hgrn_blocks(_silu(qr), g, kf, vr, sts, chain_head, m1, m2, cs_ref)
        on = o * lax.rsqrt(jnp.mean(o * o, axis=-1, keepdims=True) + EPS) * gn
        y = (on * _silu(zr)).astype(y_ref.dtype)
        for c, (u, p) in enumerate(chains):
            y_ref[0, pl.ds(r0s[u], rows), p * LANE:(p + 1) * LANE] = y[c * HG_L:c * HG_L + rows]
        return tuple(sts)

    sts = lax.fori_loop(0, n_sub // unroll, body, tuple(st_ref[p] for p in range(HG_HP)))
    for p in range(HG_HP):
        st_ref[p] = sts[p]

    @pl.when(tb == pl.num_programs(2) - 1)
    def _():
        for p in range(HG_HP):
            s_out_ref[0, p] = sts[p].T


def _hgrn(h3, lb, gain, s0, layer, masks, col0, d_b):
    bsz, t, _ = h3.shape
    n_heads = d_b // DK_B
    assert col0 % (HG_HP * LANE) == 0 and n_heads % HG_HP == 0 and DK_B == LANE
    assert s0.shape[1:] == (bsz, n_heads, DK_B, LANE)
    if t % HG_L == 0:
        rows, tb = HG_L, min(HG_TB, t)
    else:
        assert t < HG_L and t % 16 == 0
        rows, tb = t, t
    n_sub = tb // rows
    unroll = HG_UNROLL if n_sub % HG_UNROLL == 0 else 1
    assert t % tb == 0
    w = HG_HP * LANE
    c0 = col0 // w
    col = lambda seg: pl.BlockSpec((1, tb, w), lambda b, hp, i: (b, i, c0 + seg * (n_heads // HG_HP) + hp))
    per_head = pl.BlockSpec((HG_HP, 1, LANE), lambda b, hp, i: (hp, 0, 0))
    m1, m2 = masks
    y, s_new = pl.pallas_call(
        functools.partial(_hgrn_kernel, rows=rows, n_sub=n_sub, unroll=unroll),
        out_shape=(jax.ShapeDtypeStruct((bsz, t, d_b), BF16),
                   jax.ShapeDtypeStruct(s0.shape[1:], F32)),
        grid=(bsz, n_heads // HG_HP, t // tb),
        in_specs=[col(0), col(1), col(2), col(3), per_head, per_head,
                  pl.BlockSpec((None, 1, HG_HP, DK_B, LANE), lambda b, hp, i: (layer, b, hp, 0, 0)),
                  pl.BlockSpec(m1.shape, lambda b, hp, i: (0, 0)),
                  pl.BlockSpec(m2.shape, lambda b, hp, i: (0, 0))],
        out_specs=(pl.BlockSpec((1, tb, w), lambda b, hp, i: (b, i, hp)),
                   pl.BlockSpec((1, HG_HP, DK_B, LANE), lambda b, hp, i: (b, hp, 0, 0))),
        scratch_shapes=[pltpu.VMEM((HG_HP, LANE, DK_B), F32),
                        pltpu.VMEM((2, unroll * HG_HP * HG_L, LANE), F32)],
        compiler_params=_params("parallel", "parallel", "arbitrary"),
        name="hgrn2",
    )(h3, h3, h3, h3, lb.reshape(n_heads, 1, DK_B), gain.reshape(n_heads, 1, LANE).astype(F32),
      s0, m1, m2)
    return y, s_new


def _layer(x, layer, attend, s0, s0_layer, mem_args, norm_g, w_in, lb, hgrn_g, w_branch, w_out, masks, dims):
    d_a, d_b, d_c = dims
    bsz, t, d = x.shape
    m = bsz * t
    x2 = x.reshape(m, d)
    xn = _rmsnorm(x2, norm_g, BF16, 256)
    h = _matmul(xn, w_in, layer, BF16, 1024, 1024)
    h3 = h.reshape(bsz, t, -1)
    ya = attend(h3)
    yb, s_new = _hgrn(h3, lb, hgrn_g, s0, s0_layer, masks, 4 * d_a, d_b)
    yc = _mem_attn(h3, *mem_args, d_c, 4 * d_a + 4 * d_b, 512)
    merged = _merge(ya.reshape(m, d_a), yb.reshape(m, d_b), yc.reshape(m, d_c), h, w_branch, layer,
                    4 * d_a + 4 * d_b + 2 * d_c, 1024, 512)
    x_new = _matmul(merged, w_out, layer, F32, 1024, 512, res=x2).reshape(bsz, t, d)
    ka = h3[:, :, d_a:2 * d_a]
    va = h3[:, :, 2 * d_a:3 * d_a]
    return x_new, ka, va, s_new


def kernel(x_prompt, x_sample, mem_prompt, cache_attn_k, cache_attn_v, state_hgrn, cache_mem_k, cache_mem_v, norm_gain, w_in, rel_bias, lb_logits, hgrn_norm_gain, mem_norm_gain, w_mem_kv, w_branch, w_out, final_norm_gain):
    depth = w_in.shape[0]
    bp, tp, d = x_prompt.shape
    bs, ts, _ = x_sample.shape
    h_a = rel_bias.shape[1]
    d_a = h_a * DH_A
    d_b = lb_logits.shape[1]
    d_c = w_mem_kv.shape[2] // 2
    h_b = d_b // DK_B
    n_mem = mem_prompt.shape[1]
    dims = (d_a, d_b, d_c)
    ca_s = cache_attn_k.shape[2]
    ca_p = min(BAND, tp)

    w_in_b = w_in.astype(BF16)
    w_branch_b = w_branch.astype(BF16)
    w_out_b = w_out.astype(BF16)
    w_mem_b = w_mem_kv.astype(BF16)

    lb_all = _lower_bounds(lb_logits)
    masks = _hgrn_masks()
    bias_p = _prompt_bias(rel_bias)
    bias_c, bias_n = _sample_bias(rel_bias, ts, ca_s)
    s0_p = jnp.zeros((1, bp, h_b, DK_B, d_b // h_b), F32)
    mem2 = mem_prompt.reshape(bp * n_mem, d)

    xp, xs = x_prompt, x_sample
    kp_l, vp_l, sp_l, mkp_l, mvp_l, ks_l, vs_l, ss_l = [], [], [], [], [], [], [], []
    for l in range(depth):
        shared = (norm_gain[l], w_in_b, lb_all[l], hgrn_norm_gain[l], w_branch_b, w_out_b, masks, dims)

        memn = _rmsnorm(mem2, mem_norm_gain[l], BF16, 256)
        mkv = _matmul(memn, w_mem_b, l, F32, 1024, 1024).reshape(bp, n_mem, 2 * d_c)
        mem_p = (mkv, mkv,
                 pl.BlockSpec((1, n_mem, d_c), lambda b, i: (b, 0, 0)),
                 pl.BlockSpec((1, n_mem, d_c), lambda b, i: (b, 0, 1)), False)
        att_p = lambda h3: _attn_prompt(h3, bias_p, l, d_a)
        xp, ka, va, s_fin = _layer(xp, l, att_p, s0_p, 0, mem_p, *shared)
        kp_l.append(ka[:, -ca_p:].astype(F32).reshape(bp, ca_p, h_a, DH_A))
        vp_l.append(va[:, -ca_p:].astype(F32).reshape(bp, ca_p, h_a, DH_A))
        sp_l.append(s_fin)
        mkp_l.append(mkv[:, :, :d_c].reshape(bp, n_mem, H_C, d_c // H_C))
        mvp_l.append(mkv[:, :, d_c:].reshape(bp, n_mem, H_C, d_c // H_C))

        mem_blk = pl.BlockSpec((None, 1, n_mem, H_C, d_c // H_C), lambda b, i: (l, b, 0, 0, 0))
        mem_s = (cache_mem_k, cache_mem_v, mem_blk, mem_blk, True)
        att_s = lambda h3: _attn_sample(h3, cache_attn_k, cache_attn_v, bias_c, bias_n, l, d_a)
        xs, ka_s, va_s, s_new = _layer(xs, l, att_s, state_hgrn, l, mem_s, *shared)
        ks_l.append(ka_s.astype(F32).reshape(bs, ts, h_a, DH_A))
        vs_l.append(va_s.astype(F32).reshape(bs, ts, h_a, DH_A))
        ss_l.append(s_new)

    y_prompt = _rmsnorm(xp.reshape(bp * tp, d), final_norm_gain, F32, 256).reshape(bp, tp, d)
    y_sample = _rmsnorm(xs.reshape(bs * ts, d), final_norm_gain, F32, 256).reshape(bs, ts, d)
    return (y_prompt, y_sample,
            jnp.stack(kp_l), jnp.stack(vp_l), jnp.stack(sp_l), jnp.stack(mkp_l), jnp.stack(mvp_l),
            jnp.stack(ks_l), jnp.stack(vs_l), jnp.stack(ss_l))
```

```python
import functools

import numpy as np
import jax
import jax.numpy as jnp
from jax import lax
from jax.experimental import pallas as pl
from jax.experimental.pallas import tpu as pltpu

F32 = jnp.float32
BF16 = jnp.bfloat16

PAST_LEN = 4096
CHUNK = 64
N_LEFT_CHUNKS = 8
BAND = N_LEFT_CHUNKS * CHUNK
DH_A = 128
MAX_REL = 256
DK_B = 128
H_C = 4
EPS = 1e-6
NEG = -1e30
LOG2E = 1.4426950408889634

LANE = 128
VMEM_LIMIT_BYTES = 56 * 1024 * 1024

ATT_TQ = 256
ATT_NKB = BAND // ATT_TQ + 1
HG_L = 64
HG_TB = 512
HG_HP = 4
HG_UNROLL = 2
HG_GROUPS = 4

NT_DIMS = (((1,), (1,)), ((), ()))
TN_DIMS = (((0,), (0,)), ((), ()))


def _params(*sem, flags=None):
    return pltpu.CompilerParams(dimension_semantics=sem, vmem_limit_bytes=VMEM_LIMIT_BYTES, flags=flags)


def _silu(z):
    return z * jax.nn.sigmoid(z)


def _rmsnorm_kernel(x_ref, g_ref, o_ref):
    x = x_ref[...].astype(F32)
    y = x * lax.rsqrt(jnp.mean(x * x, axis=-1, keepdims=True) + EPS)
    o_ref[...] = (y * g_ref[...]).astype(o_ref.dtype)


def _rmsnorm(x, g, out_dtype, tm):
    m, d = x.shape
    tm = min(tm, m)
    return pl.pallas_call(
        _rmsnorm_kernel,
        out_shape=jax.ShapeDtypeStruct((m, d), out_dtype),
        grid=(m // tm,),
        in_specs=[pl.BlockSpec((tm, d), lambda i: (i, 0)),
                  pl.BlockSpec((1, d), lambda i: (0, 0))],
        out_specs=pl.BlockSpec((tm, d), lambda i: (i, 0)),
        compiler_params=_params("parallel"),
        name="rmsnorm",
    )(x, g.reshape(1, d).astype(F32))


def _mm_kernel(a_ref, b_ref, o_ref):
    o_ref[...] = jnp.dot(a_ref[...], b_ref[...], preferred_element_type=F32).astype(o_ref.dtype)


def _mm_res_kernel(a_ref, b_ref, r_ref, o_ref):
    o_ref[...] = r_ref[...] + jnp.dot(a_ref[...], b_ref[...], preferred_element_type=F32)


def _matmul(a, w, layer, out_dtype, tm, tn, res=None):
    m, k = a.shape
    n = w.shape[2]
    tm, tn = min(tm, m), min(tn, n)
    in_specs = [pl.BlockSpec((tm, k), lambda i, j: (i, 0)),
                pl.BlockSpec((None, k, tn), lambda i, j: (layer, 0, j))]
    args = [a, w]
    kern = _mm_kernel
    if res is not None:
        in_specs.append(pl.BlockSpec((tm, tn), lambda i, j: (i, j)))
        args.append(res)
        kern = _mm_res_kernel
    return pl.pallas_call(
        kern,
        out_shape=jax.ShapeDtypeStruct((m, n), out_dtype),
        grid=(m // tm, n // tn),
        in_specs=in_specs,
        out_specs=pl.BlockSpec((tm, tn), lambda i, j: (i, j)),
        compiler_params=_params("parallel", "arbitrary"),
        name="matmul_res" if res is not None else "matmul",
    )(*args)


def _merge_kernel(ya_ref, yb_ref, yc_ref, wa_ref, wb_ref, wc_ref, ga_ref, gb_ref, gc_ref, o_ref):
    def part(y_ref, w_ref, g_ref):
        gate = jax.nn.sigmoid(g_ref[...].astype(F32))
        return gate * jnp.dot(y_ref[...], w_ref[...], preferred_element_type=F32)

    o_ref[...] = (part(ya_ref, wa_ref, ga_ref) + part(yb_ref, wb_ref, gb_ref)
                  + part(yc_ref, wc_ref, gc_ref)).astype(o_ref.dtype)


def _merge(ya, yb, yc, h, w_branch, layer, gate_col0, tm, tn):
    m, d_a = ya.shape
    d_b, d_c = yb.shape[1], yc.shape[1]
    d = w_branch.shape[2]
    tm = min(tm, m)
    assert d_a == d_b and (d_a + d_b) % d_c == 0 and gate_col0 % tn == 0 and d % tn == 0
    g0, gstep = gate_col0 // tn, d // tn
    return pl.pallas_call(
        _merge_kernel,
        out_shape=jax.ShapeDtypeStruct((m, d), BF16),
        grid=(m // tm, d // tn),
        in_specs=[pl.BlockSpec((tm, d_a), lambda i, j: (i, 0)),
                  pl.BlockSpec((tm, d_b), lambda i, j: (i, 0)),
                  pl.BlockSpec((tm, d_c), lambda i, j: (i, 0)),
                  pl.BlockSpec((None, d_a, tn), lambda i, j: (layer, 0, j)),
                  pl.BlockSpec((None, d_b, tn), lambda i, j: (layer, 1, j)),
                  pl.BlockSpec((None, d_c, tn), lambda i, j: (layer, (d_a + d_b) // d_c, j)),
                  pl.BlockSpec((tm, tn), lambda i, j: (i, g0 + j)),
                  pl.BlockSpec((tm, tn), lambda i, j: (i, g0 + gstep + j)),
                  pl.BlockSpec((tm, tn), lambda i, j: (i, g0 + 2 * gstep + j))],
        out_specs=pl.BlockSpec((tm, tn), lambda i, j: (i, j)),
        compiler_params=_params("parallel", "arbitrary"),
        name="merge",
    )(ya, yb, yc, w_branch, w_branch, w_branch, h, h, h)


def _lower_bound_kernel(x_ref, o_ref):
    x = x_ref[...].astype(F32)
    e = jnp.exp(x - jnp.max(x, axis=0, keepdims=True))
    sm = e / jnp.sum(e, axis=0, keepdims=True)
    row = lax.broadcasted_iota(jnp.int32, x.shape, 0)
    acc = jnp.zeros_like(x)
    for i in range(1, x.shape[0]):
        acc = acc + jnp.where(row >= i, sm[i:i + 1, :], 0.0)
    o_ref[...] = acc


def _lower_bounds(lb_logits):
    return pl.pallas_call(
        _lower_bound_kernel,
        out_shape=jax.ShapeDtypeStruct(lb_logits.shape, F32),
        name="hgrn_lower_bound",
    )(lb_logits)


def _toeplitz_kernel(w_ref, mask_ref, o_ref, *, mult):
    rows, cols = o_ref.shape[-2:]
    p = w_ref.shape[-1]
    x = jnp.broadcast_to(w_ref[0, 0], (rows, p))
    x = pltpu.roll(x, 0, 1, stride=1, stride_axis=0)
    o_ref[0, 0] = x[:, :cols] * mult + mask_ref[0]


def _rel_bias_blocks(rel_bias, rows, cols, deltas, masks):
    p = -(-(rows + cols - 1) // LANE) * LANE
    k = np.arange(p)
    diag = np.where(k < cols, -k, p - k)
    idx = np.stack([np.clip(diag + dl, -MAX_REL, MAX_REL) + MAX_REL for dl in deltas])
    n_tab = rel_bias.shape[0] * rel_bias.shape[1]
    w = jnp.take(rel_bias.astype(F32).reshape(n_tab, -1), jnp.asarray(idx.reshape(-1)), axis=1)
    w = w.reshape(n_tab, len(deltas), 1, p)
    return pl.pallas_call(
        functools.partial(_toeplitz_kernel, mult=DH_A ** 0.5),
        out_shape=jax.ShapeDtypeStruct((n_tab, len(deltas), rows, cols), F32),
        grid=(n_tab, len(deltas)),
        in_specs=[pl.BlockSpec((1, 1, 1, p), lambda i, j: (i, j, 0, 0)),
                  pl.BlockSpec((1, rows, cols), lambda i, j: (j, 0, 0))],
        out_specs=pl.BlockSpec((1, 1, rows, cols), lambda i, j: (i, j, 0, 0)),
        compiler_params=_params("parallel", "parallel"),
        name="rel_bias_blocks",
    )(w, jnp.asarray(masks, F32))


def _band_mask(q_pos, k_pos):
    qc, kc = q_pos // CHUNK, k_pos // CHUNK
    valid = (kc[None, :] <= qc[:, None]) & (kc[None, :] >= qc[:, None] - N_LEFT_CHUNKS)
    return np.where(valid, 0.0, NEG)


def _prompt_bias(rel_bias):
    r = np.arange(ATT_TQ)
    deltas = [(ATT_NKB - 1 - j) * ATT_TQ for j in range(ATT_NKB)]
    base = BAND
    masks = np.stack([_band_mask(base + r, base - dl + r) for dl in deltas])
    return _rel_bias_blocks(rel_bias, ATT_TQ, ATT_TQ, deltas, masks)


def _sample_bias(rel_bias, t, ca):
    q_pos = PAST_LEN + np.arange(t)
    k_pos = np.concatenate([PAST_LEN - ca + np.arange(ca), q_pos])
    mask = _band_mask(q_pos, k_pos) + np.where(k_pos >= 0, 0.0, NEG)[None, :]
    b = _rel_bias_blocks(rel_bias, t, ca + t, [ca], mask[None])[:, 0]
    return b[:, :, :ca], b[:, :, ca:]


def _softmax_pv(scores, values, c, ones_col):
    if all(s.shape == scores[0].shape for s in scores):
        mx = scores[0]
        for s in scores[1:]:
            mx = jnp.maximum(mx, s)
        m = jnp.max(mx, axis=-1, keepdims=True)
    else:
        m = jnp.max(scores[0], axis=-1, keepdims=True)
        for s in scores[1:]:
            m = jnp.maximum(m, jnp.max(s, axis=-1, keepdims=True))
    dh = values[0].shape[1]
    l = None
    o = None
    for s, v in zip(scores, values):
        p = jnp.exp2((s - m) * c).astype(BF16)
        if ones_col:
            v = jnp.concatenate([v, jnp.ones_like(v)], axis=1)
        else:
            ps = jnp.sum(p.astype(F32), axis=-1, keepdims=True)
            l = ps if l is None else l + ps
        pv = jnp.dot(p, v, preferred_element_type=F32)
        o = pv if o is None else o + pv
    if ones_col:
        return o[:, :dh] / o[:, dh:]
    return o / l


def _attn_prompt_kernel(q_ref, *refs, n_heads, dh, nkb, scale):
    k_refs = refs[:nkb]
    v_refs = refs[nkb:2 * nkb]
    z_ref, bias_ref, o_ref = refs[2 * nkb:]
    t = pl.program_id(1)
    for hd in range(n_heads):
        sl = slice(hd * dh, (hd + 1) * dh)
        q = q_ref[0, :, sl]
        scores = []
        for j in range(nkb):
            s = lax.dot_general(q, k_refs[j][0, :, sl], NT_DIMS, preferred_element_type=F32)
            s = s + bias_ref[hd, j]
            if j < nkb - 1:
                s = jnp.where(t >= nkb - 1 - j, s, NEG)
            scores.append(s)
        o = _softmax_pv(scores, [v_refs[j][0, :, sl] for j in range(nkb)], scale * LOG2E, True)
        z = z_ref[0, :, sl].astype(F32)
        o_ref[0, :, sl] = (o * _silu(z)).astype(o_ref.dtype)


def _attn_prompt(h3, bias, layer, d_a):
    bsz, t, _ = h3.shape
    n_heads = d_a // DH_A
    nkb = ATT_NKB
    assert t % ATT_TQ == 0 and ATT_TQ % CHUNK == 0 and BAND % ATT_TQ == 0

    def kv_spec(col, j):
        return pl.BlockSpec((1, ATT_TQ, d_a),
                            lambda b, i: (b, jnp.maximum(i - (nkb - 1 - j), 0), col))

    in_specs = ([pl.BlockSpec((1, ATT_TQ, d_a), lambda b, i: (b, i, 0))]
                + [kv_spec(1, j) for j in range(nkb)]
                + [kv_spec(2, j) for j in range(nkb)]
                + [pl.BlockSpec((1, ATT_TQ, d_a), lambda b, i: (b, i, 3)),
                   pl.BlockSpec((n_heads,) + bias.shape[1:], lambda b, i: (layer, 0, 0, 0))])
    return pl.pallas_call(
        functools.partial(_attn_prompt_kernel, n_heads=n_heads, dh=DH_A, nkb=nkb, scale=DH_A ** -0.5),
        out_shape=jax.ShapeDtypeStruct((bsz, t, d_a), BF16),
        grid=(bsz, t // ATT_TQ),
        in_specs=in_specs,
        out_specs=pl.BlockSpec((1, ATT_TQ, d_a), lambda b, i: (b, i, 0)),
        compiler_params=_params("parallel", "arbitrary"),
        name="attn_prompt",
    )(h3, *([h3] * (2 * nkb)), h3, bias)


def _attn_sample_kernel(q_ref, k_ref, v_ref, z_ref, kc_ref, vc_ref, bc_ref, bn_ref, o_ref, *, n_heads, dh, scale):
    for hd in range(n_heads):
        sl = slice(hd * dh, (hd + 1) * dh)
        q = q_ref[0, :, sl]
        s_c = lax.dot_general(q, kc_ref[0, hd].astype(BF16), NT_DIMS, preferred_element_type=F32)
        s_n = lax.dot_general(q, k_ref[0, :, sl], NT_DIMS, preferred_element_type=F32)
        scores = [s_c + bc_ref[hd], s_n + bn_ref[hd]]
        o = _softmax_pv(scores, [vc_ref[0, hd].astype(BF16), v_ref[0, :, sl]], scale * LOG2E, True)
        z = z_ref[0, :, sl].astype(F32)
        o_ref[0, :, sl] = (o * _silu(z)).astype(o_ref.dtype)


def _attn_sample(h3, k_cache, v_cache, bias_c, bias_n, layer, d_a):
    bsz, t, _ = h3.shape
    ca = k_cache.shape[3]
    n_heads = d_a // DH_A
    assert k_cache.shape[2:] == (n_heads, ca, DH_A)
    row = lambda col: pl.BlockSpec((1, t, d_a), lambda b: (b, 0, col))
    cache = pl.BlockSpec((None, 1, n_heads, ca, DH_A), lambda b: (layer, b, 0, 0, 0))
    bias = lambda a: pl.BlockSpec((n_heads,) + a.shape[1:], lambda b: (layer, 0, 0))
    return pl.pallas_call(
        functools.partial(_attn_sample_kernel, n_heads=n_heads, dh=DH_A, scale=DH_A ** -0.5),
        out_shape=jax.ShapeDtypeStruct((bsz, t, d_a), BF16),
        grid=(bsz,),
        in_specs=[row(0), row(1), row(2), row(3), cache, cache, bias(bias_c), bias(bias_n)],
        out_specs=pl.BlockSpec((1, t, d_a), lambda b: (b, 0, 0)),
        compiler_params=_params("parallel"),
        name="attn_sample",
    )(h3, h3, h3, h3, k_cache, v_cache, bias_c, bias_n)


def _mem_attn_kernel(q_ref, z_ref, mk_ref, mv_ref, o_ref, *, n_heads, dh, scale, head_axis):
    for hd in range(n_heads):
        sl = slice(hd * dh, (hd + 1) * dh)
        mk = mk_ref[0, :, hd, :] if head_axis else mk_ref[0, :, sl]
        mv = mv_ref[0, :, hd, :] if head_axis else mv_ref[0, :, sl]
        s = lax.dot_general(q_ref[0, :, sl], mk.astype(BF16), NT_DIMS, preferred_element_type=F32)
        o = _softmax_pv([s], [mv.astype(BF16)], scale * LOG2E, False)
        z = z_ref[0, :, sl].astype(F32)
        o_ref[0, :, sl] = (o * _silu(z)).astype(o_ref.dtype)


def _mem_attn(h3, mk, mv, mk_spec, mv_spec, head_axis, d_c, q_col0, tq):
    bsz, t, _ = h3.shape
    tq = min(tq, t)
    assert q_col0 % d_c == 0
    qb = q_col0 // d_c
    dh = d_c // H_C
    return pl.pallas_call(
        functools.partial(_mem_attn_kernel, n_heads=H_C, dh=dh, scale=dh ** -0.5, head_axis=head_axis),
        out_shape=jax.ShapeDtypeStruct((bsz, t, d_c), BF16),
        grid=(bsz, t // tq),
        in_specs=[pl.BlockSpec((1, tq, d_c), lambda b, i: (b, i, qb)),
                  pl.BlockSpec((1, tq, d_c), lambda b, i: (b, i, qb + 1)),
                  mk_spec, mv_spec],
        out_specs=pl.BlockSpec((1, tq, d_c), lambda b, i: (b, i, 0)),
        compiler_params=_params("parallel", "arbitrary"),
        name="mem_attn",
    )(h3, h3, mk, mv)


def _seg_cumsum(x, row, group):
    d = 1
    while d < group:
        x = x + jnp.where((row & (group - 1)) >= d, pltpu.roll(x, d, 0), 0.0)
        d *= 2
    return x


def _hgrn_masks():
    t = np.arange(HG_L)[:, None]
    s = np.arange(HG_L)[None, :]
    m16 = [(t // 16 == i) & (s < 16 * i) for i in range(1, 4)]
    m4 = [(t // 16 == s // 16) & ((t % 16) // 4 == i) & (s % 16 < 4 * i) for i in range(1, 4)]
    base = (t // 4 == s // 4) & (s <= t)
    m1 = np.concatenate(m16, axis=1).astype(np.float32)
    m2 = np.concatenate(m4 + [base], axis=1).astype(np.float32)
    return jnp.asarray(m1), jnp.asarray(m2)


def _bcast_rows(ref, offsets, reps):
    return jnp.concatenate(
        [jnp.broadcast_to(ref[o:o + 1, :], (reps, LANE)) for o in offsets], axis=0)


def _hgrn_blocks(q, g, k, v, sts, chain_head, m1, m2, cs_ref):
    L = HG_L
    n = q.shape[0] // L
    starts = [c * L for c in range(n)]
    row = lax.broadcasted_iota(jnp.int32, q.shape, 0)
    r64 = row & (L - 1)
    c4 = _seg_cumsum(g, row, 4)
    c16 = _seg_cumsum(g, row, 16)
    c16_ref, b_ref = cs_ref.at[0], cs_ref.at[1]
    c16_ref[...] = c16
    b = c16
    for j in range(1, L // 16):
        b = b + jnp.where(r64 >= 16 * j, _bcast_rows(c16_ref, [s + 16 * j - 1 for s in starts], L), 0.0)
    b_ref[...] = b

    q4 = (q * jnp.exp2(c4)).astype(BF16)
    q16 = (q * jnp.exp2(c16)).astype(BF16)
    q64 = (q * jnp.exp2(b)).astype(BF16)

    k16 = []
    for i in range(1, L // 16):
        ref_rows = _bcast_rows(b_ref, [s + 16 * i - 1 for s in starts], L)
        k16.append((k * jnp.exp2(jnp.minimum(ref_rows - b, 0.0))).astype(BF16))
    k4 = []
    for i in range(1, 4):
        ref_rows = _bcast_rows(c16_ref, [s + 16 * j + 4 * i - 1 for s in starts for j in range(L // 16)], 16)
        k4.append((k * jnp.exp2(jnp.minimum(ref_rows - c16, 0.0))).astype(BF16))
    k4.append((k * jnp.exp2(-c4)).astype(BF16))
    kd = (k * jnp.exp2(_bcast_rows(b_ref, [s + L - 1 for s in starts], L) - b)).astype(BF16)
    vb = v.astype(BF16)

    scores = []
    for s in starts:
        sl = slice(s, s + L)
        a1 = lax.dot_general(q16[sl], jnp.concatenate([x[sl] for x in k16], axis=0), NT_DIMS,
                             preferred_element_type=F32) * m1
        a2 = lax.dot_general(q4[sl], jnp.concatenate([x[sl] for x in k4], axis=0), NT_DIMS,
                             preferred_element_type=F32) * m2
        scores.append((a1.astype(BF16), a2.astype(BF16)))
    sts = list(sts)
    outs = []
    for c, s in enumerate(starts):
        sl = slice(s, s + L)
        a1, a2 = scores[c]
        st = sts[chain_head[c]]
        vc = vb[sl]
        outs.append(
            jnp.dot(a1, jnp.concatenate([vc] * 3, axis=0), preferred_element_type=F32)
            + jnp.dot(a2, jnp.concatenate([vc] * 4, axis=0), preferred_element_type=F32)
            + lax.dot_general(q64[sl], st.astype(BF16), NT_DIMS, preferred_element_type=F32))
        sts[chain_head[c]] = (st * jnp.exp2(b_ref[s + L - 1:s + L, :])
                              + lax.dot_general(vc, kd[sl], TN_DIMS, preferred_element_type=F32))
    return jnp.concatenate(outs, axis=0), sts


def _hgrn_kernel(q_ref, f_ref, i_ref, z_ref, lb_ref, gn_ref, s0_ref, m1_ref, m2_ref,
                 y_ref, s_out_ref, st_ref, cs_ref, *, rows, n_sub, unroll, groups):
    tb = pl.program_id(2)

    @pl.when(tb == 0)
    def _():
        for p in range(HG_HP):
            st_ref[p] = s0_ref[0, p].astype(F32).T

    m1 = m1_ref[...]
    m2 = m2_ref[...]
    chains = [(u, p) for u in range(unroll) for p in range(HG_HP)]
    chain_head = [p for _, p in chains]
    lb = jnp.concatenate([jnp.broadcast_to(lb_ref[p], (HG_L, LANE)) for _, p in chains], axis=0)
    gn = jnp.concatenate([jnp.broadcast_to(gn_ref[p], (HG_L, LANE)) for _, p in chains], axis=0)

    def stack(ref, r0s):
        parts = []
        for u, p in chains:
            x = ref[0, pl.ds(r0s[u], rows), p * LANE:(p + 1) * LANE].astype(F32)
            if rows < HG_L:
                x = jnp.concatenate([x, jnp.zeros((HG_L - rows, LANE), F32)], axis=0)
            parts.append(x)
        return jnp.concatenate(parts, axis=0)

    def group(blk0, sts, scratch):
        r0s = [pl.multiple_of((blk0 + u) * rows, rows) for u in range(unroll)]
        qr, fr, vr, zr = (stack(ref, r0s) for ref in (q_ref, f_ref, i_ref, z_ref))
        kf = (1.0 - lb) * jax.nn.sigmoid(-fr)
        g = jnp.log1p(-kf) * LOG2E
        if rows < HG_L:
            live = (lax.broadcasted_iota(jnp.int32, fr.shape, 0) & (HG_L - 1)) < rows
            kf = jnp.where(live, kf, 0.0)
            g = jnp.where(live, g, 0.0)
        o, sts = _hgrn_blocks(_silu(qr), g, kf, vr, sts, chain_head, m1, m2, scratch)
        on = o * lax.rsqrt(jnp.mean(o * o, axis=-1, keepdims=True) + EPS) * gn
        y = (on * _silu(zr)).astype(y_ref.dtype)
        for c, (u, p) in enumerate(chains):
            y_ref[0, pl.ds(r0s[u], rows), p * LANE:(p + 1) * LANE] = y[c * HG_L:c * HG_L + rows]
        return sts

    def body(it, sts):
        for gi in range(groups):
            sts = group((it * groups + gi) * unroll, sts, cs_ref.at[gi])
        return tuple(sts)

    sts = lax.fori_loop(0, n_sub // (unroll * groups), body, tuple(st_ref[p] for p in range(HG_HP)))
    for p in range(HG_HP):
        st_ref[p] = sts[p]

    @pl.when(tb == pl.num_programs(2) - 1)
    def _():
        for p in range(HG_HP):
            s_out_ref[0, p] = sts[p].T


def _hgrn(h3, lb, gain, s0, layer, masks, col0, d_b):
    bsz, t, _ = h3.shape
    n_heads = d_b // DK_B
    assert col0 % (HG_HP * LANE) == 0 and n_heads % HG_HP == 0 and DK_B == LANE
    assert s0.shape[1:] == (bsz, n_heads, DK_B, LANE)
    if t % HG_L == 0:
        rows, tb = HG_L, min(HG_TB, t)
    else:
        assert t < HG_L and t % 16 == 0
        rows, tb = t, t
    n_sub = tb // rows
    unroll = HG_UNROLL if n_sub % HG_UNROLL == 0 else 1
    groups = HG_GROUPS if n_sub % (unroll * HG_GROUPS) == 0 else 1
    assert t % tb == 0
    w = HG_HP * LANE
    c0 = col0 // w
    col = lambda seg: pl.BlockSpec((1, tb, w), lambda b, hp, i: (b, i, c0 + seg * (n_heads // HG_HP) + hp))
    per_head = pl.BlockSpec((HG_HP, 1, LANE), lambda b, hp, i: (hp, 0, 0))
    m1, m2 = masks
    y, s_new = pl.pallas_call(
        functools.partial(_hgrn_kernel, rows=rows, n_sub=n_sub, unroll=unroll, groups=groups),
        out_shape=(jax.ShapeDtypeStruct((bsz, t, d_b), BF16),
                   jax.ShapeDtypeStruct(s0.shape[1:], F32)),
        grid=(bsz, n_heads // HG_HP, t // tb),
        in_specs=[col(0), col(1), col(2), col(3), per_head, per_head,
                  pl.BlockSpec((None, 1, HG_HP, DK_B, LANE), lambda b, hp, i: (layer, b, hp, 0, 0)),
                  pl.BlockSpec(m1.shape, lambda b, hp, i: (0, 0)),
                  pl.BlockSpec(m2.shape, lambda b, hp, i: (0, 0))],
        out_specs=(pl.BlockSpec((1, tb, w), lambda b, hp, i: (b, i, hp)),
                   pl.BlockSpec((1, HG_HP, DK_B, LANE), lambda b, hp, i: (b, hp, 0, 0))),
        scratch_shapes=[pltpu.VMEM((HG_HP, LANE, DK_B), F32),
                        pltpu.VMEM((groups, 2, unroll * HG_HP * HG_L, LANE), F32)],
        compiler_params=_params("parallel", "parallel", "arbitrary"),
        name="hgrn2",
    )(h3, h3, h3, h3, lb.reshape(n_heads, 1, DK_B), gain.reshape(n_heads, 1, LANE).astype(F32),
      s0, m1, m2)
    return y, s_new


def _layer(x, layer, attend, s0, s0_layer, mem_args, norm_g, w_in, lb, hgrn_g, w_branch, w_out, masks, dims):
    d_a, d_b, d_c = dims
    bsz, t, d = x.shape
    m = bsz * t
    x2 = x.reshape(m, d)
    xn = _rmsnorm(x2, norm_g, BF16, 256)
    h = _matmul(xn, w_in, layer, BF16, 1024, 1024)
    h3 = h.reshape(bsz, t, -1)
    ya = attend(h3)
    yb, s_new = _hgrn(h3, lb, hgrn_g, s0, s0_layer, masks, 4 * d_a, d_b)
    yc = _mem_attn(h3, *mem_args, d_c, 4 * d_a + 4 * d_b, 512)
    merged = _merge(ya.reshape(m, d_a), yb.reshape(m, d_b), yc.reshape(m, d_c), h, w_branch, layer,
                    4 * d_a + 4 * d_b + 2 * d_c, 1024, 1024)
    x_new = _matmul(merged, w_out, layer, F32, 1024, 1024, res=x2).reshape(bsz, t, d)
    ka = h3[:, :, d_a:2 * d_a]
    va = h3[:, :, 2 * d_a:3 * d_a]
    return x_new, ka, va, s_new


def kernel(x_prompt, x_sample, mem_prompt, cache_attn_k, cache_attn_v, state_hgrn, cache_mem_k, cache_mem_v, norm_gain, w_in, rel_bias, lb_logits, hgrn_norm_gain, mem_norm_gain, w_mem_kv, w_branch, w_out, final_norm_gain):
    depth = w_in.shape[0]
    bp, tp, d = x_prompt.shape
    bs, ts, _ = x_sample.shape
    h_a = rel_bias.shape[1]
    d_a = h_a * DH_A
    d_b = lb_logits.shape[1]
    d_c = w_mem_kv.shape[2] // 2
    h_b = d_b // DK_B
    n_mem = mem_prompt.shape[1]
    dims = (d_a, d_b, d_c)
    ca_s = cache_attn_k.shape[2]
    ca_p = min(BAND, tp)

    w_in_b = w_in.astype(BF16)
    w_branch_b = w_branch.astype(BF16)
    w_out_b = w_out.astype(BF16)
    w_mem_b = w_mem_kv.astype(BF16)

    kc_s = jnp.swapaxes(cache_attn_k, 2, 3)
    vc_s = jnp.swapaxes(cache_attn_v, 2, 3)

    lb_all = _lower_bounds(lb_logits)
    masks = _hgrn_masks()
    bias_p = _prompt_bias(rel_bias)
    bias_c, bias_n = _sample_bias(rel_bias, ts, ca_s)
    s0_p = jnp.zeros((1, bp, h_b, DK_B, d_b // h_b), F32)
    mem2 = mem_prompt.reshape(bp * n_mem, d)

    xp, xs = x_prompt, x_sample
    kp_l, vp_l, sp_l, mkp_l, mvp_l, ks_l, vs_l, ss_l = [], [], [], [], [], [], [], []
    for l in range(depth):
        shared = (norm_gain[l], w_in_b, lb_all[l], hgrn_norm_gain[l], w_branch_b, w_out_b, masks, dims)

        memn = _rmsnorm(mem2, mem_norm_gain[l], BF16, 256)
        mkv = _matmul(memn, w_mem_b, l, F32, 1024, 1024).reshape(bp, n_mem, 2 * d_c)
        mem_p = (mkv, mkv,
                 pl.BlockSpec((1, n_mem, d_c), lambda b, i: (b, 0, 0)),
                 pl.BlockSpec((1, n_mem, d_c), lambda b, i: (b, 0, 1)), False)
        att_p = lambda h3: _attn_prompt(h3, bias_p, l, d_a)
        xp, ka, va, s_fin = _layer(xp, l, att_p, s0_p, 0, mem_p, *shared)
        kp_l.append(ka[:, -ca_p:].astype(F32).reshape(bp, ca_p, h_a, DH_A))
        vp_l.append(va[:, -ca_p:].astype(F32).reshape(bp, ca_p, h_a, DH_A))
        sp_l.append(s_fin)
        mkp_l.append(mkv[:, :, :d_c].reshape(bp, n_mem, H_C, d_c // H_C))
        mvp_l.append(mkv[:, :, d_c:].reshape(bp, n_mem, H_C, d_c // H_C))

        mem_blk = pl.BlockSpec((None, 1, n_mem, H_C, d_c // H_C), lambda b, i: (l, b, 0, 0, 0))
        mem_s = (cache_mem_k, cache_mem_v, mem_blk, mem_blk, True)
        att_s = lambda h3: _attn_sample(h3, kc_s, vc_s, bias_c, bias_n, l, d_a)
        xs, ka_s, va_s, s_new = _layer(xs, l, att_s, state_hgrn, l, mem_s, *shared)
        ks_l.append(ka_s.astype(F32).reshape(bs, ts, h_a, DH_A))
        vs_l.append(va_s.astype(F32).reshape(bs, ts, h_a, DH_A))
        ss_l.append(s_new)

    y_prompt = _rmsnorm(xp.reshape(bp * tp, d), final_norm_gain, F32, 256).reshape(bp, tp, d)
    y_sample = _rmsnorm(xs.reshape(bs * ts, d), final_norm_gain, F32, 256).reshape(bs, ts, d)
    return (y_prompt, y_sample,
            jnp.stack(kp_l), jnp.stack(vp_l), jnp.stack(sp_l), jnp.stack(mkp_l), jnp.stack(mvp_l),
            jnp.stack(ks_l), jnp.stack(vs_l), jnp.stack(ss_l))
```

```python
import functools

import numpy as np
import jax
import jax.numpy as jnp
from jax import lax
from jax.experimental import pallas as pl
from jax.experimental.pallas import tpu as pltpu

F32 = jnp.float32
BF16 = jnp.bfloat16

PAST_LEN = 4096
CHUNK = 64
N_LEFT_CHUNKS = 8
BAND = N_LEFT_CHUNKS * CHUNK
DH_A = 128
MAX_REL = 256
DK_B = 128
H_C = 4
EPS = 1e-6
NEG = -1e30
LOG2E = 1.4426950408889634

LANE = 128
VMEM_LIMIT_BYTES = 56 * 1024 * 1024

ATT_TQ = 256
ATT_NKB = BAND // ATT_TQ + 1
HG_L = 64
HG_TB = 512
HG_HP = 4
HG_UNROLL = 2
HG_GROUPS = 4

NT_DIMS = (((1,), (1,)), ((), ()))
TN_DIMS = (((0,), (0,)), ((), ()))


def _params(*sem, flags=None):
    return pltpu.CompilerParams(dimension_semantics=sem, vmem_limit_bytes=VMEM_LIMIT_BYTES, flags=flags)


def _silu(z):
    return z * jax.nn.sigmoid(z)


def _rmsnorm_kernel(x_ref, g_ref, o_ref):
    x = x_ref[...].astype(F32)
    y = x * lax.rsqrt(jnp.mean(x * x, axis=-1, keepdims=True) + EPS)
    o_ref[...] = (y * g_ref[...]).astype(o_ref.dtype)


def _rmsnorm(x, g, out_dtype, tm):
    m, d = x.shape
    tm = min(tm, m)
    return pl.pallas_call(
        _rmsnorm_kernel,
        out_shape=jax.ShapeDtypeStruct((m, d), out_dtype),
        grid=(m // tm,),
        in_specs=[pl.BlockSpec((tm, d), lambda i: (i, 0)),
                  pl.BlockSpec((1, d), lambda i: (0, 0))],
        out_specs=pl.BlockSpec((tm, d), lambda i: (i, 0)),
        compiler_params=_params("parallel"),
        name="rmsnorm",
    )(x, g.reshape(1, d).astype(F32))


def _mm_kernel(a_ref, b_ref, o_ref):
    o_ref[...] = jnp.dot(a_ref[...], b_ref[...], preferred_element_type=F32).astype(o_ref.dtype)


def _mm_res_kernel(a_ref, b_ref, r_ref, o_ref):
    o_ref[...] = r_ref[...] + jnp.dot(a_ref[...], b_ref[...], preferred_element_type=F32)


def _mm_cast_kernel(a_ref, b_ref, *refs):
    n_cast = (len(refs) - 1) // 2
    src_refs, o_ref, dst_refs = refs[:n_cast], refs[n_cast], refs[n_cast + 1:]
    o_ref[...] = jnp.dot(a_ref[...], b_ref[...], preferred_element_type=F32).astype(o_ref.dtype)
    for src, dst in zip(src_refs, dst_refs):
        dst[...] = src[...].astype(dst.dtype)


def _matmul(a, w, layer, out_dtype, tm, tn, res=None, casts=()):
    m, k = a.shape
    n = w.shape[2]
    tm, tn = min(tm, m), min(tn, n)
    gi, gj = m // tm, n // tn
    in_specs = [pl.BlockSpec((tm, k), lambda i, j: (i, 0)),
                pl.BlockSpec((None, k, tn), lambda i, j: (layer, 0, j))]
    args = [a, w]
    out_shape = [jax.ShapeDtypeStruct((m, n), out_dtype)]
    out_specs = [pl.BlockSpec((tm, tn), lambda i, j: (i, j))]
    kern, name = _mm_kernel, "matmul"
    if res is not None:
        assert not casts
        in_specs.append(pl.BlockSpec((tm, tn), lambda i, j: (i, j)))
        args.append(res)
        kern, name = _mm_res_kernel, "matmul_res"
    if casts:
        kern, name = _mm_cast_kernel, "matmul_cast"
    for src, src_layer, bc in casts:
        _, r, c = src.shape
        br, nj = r // gi, c // bc
        assert br * gi == r and nj * bc == c and nj <= gj
        in_specs.append(pl.BlockSpec((None, br, bc),
                                     lambda i, j, sl=src_layer, nj=nj: (sl, i, jnp.minimum(j, nj - 1))))
        args.append(src)
        out_shape.append(jax.ShapeDtypeStruct((1, r, c), BF16))
        out_specs.append(pl.BlockSpec((None, br, bc), lambda i, j, nj=nj: (0, i, jnp.minimum(j, nj - 1))))
    outs = pl.pallas_call(
        kern,
        out_shape=out_shape,
        grid=(gi, gj),
        in_specs=in_specs,
        out_specs=out_specs,
        compiler_params=_params("parallel", "arbitrary"),
        name=name,
    )(*args)
    return outs if casts else outs[0]


def _merge_kernel(ya_ref, yb_ref, yc_ref, wa_ref, wb_ref, wc_ref, ga_ref, gb_ref, gc_ref, o_ref):
    def part(y_ref, w_ref, g_ref):
        gate = jax.nn.sigmoid(g_ref[...].astype(F32))
        return gate * jnp.dot(y_ref[...], w_ref[...], preferred_element_type=F32)

    o_ref[...] = (part(ya_ref, wa_ref, ga_ref) + part(yb_ref, wb_ref, gb_ref)
                  + part(yc_ref, wc_ref, gc_ref)).astype(o_ref.dtype)


def _merge(ya, yb, yc, h, w_branch, layer, gate_col0, tm, tn):
    m, d_a = ya.shape
    d_b, d_c = yb.shape[1], yc.shape[1]
    d = w_branch.shape[2]
    tm = min(tm, m)
    assert d_a == d_b and (d_a + d_b) % d_c == 0 and gate_col0 % tn == 0 and d % tn == 0
    g0, gstep = gate_col0 // tn, d // tn
    return pl.pallas_call(
        _merge_kernel,
        out_shape=jax.ShapeDtypeStruct((m, d), BF16),
        grid=(m // tm, d // tn),
        in_specs=[pl.BlockSpec((tm, d_a), lambda i, j: (i, 0)),
                  pl.BlockSpec((tm, d_b), lambda i, j: (i, 0)),
                  pl.BlockSpec((tm, d_c), lambda i, j: (i, 0)),
                  pl.BlockSpec((None, d_a, tn), lambda i, j: (layer, 0, j)),
                  pl.BlockSpec((None, d_b, tn), lambda i, j: (layer, 1, j)),
                  pl.BlockSpec((None, d_c, tn), lambda i, j: (layer, (d_a + d_b) // d_c, j)),
                  pl.BlockSpec((tm, tn), lambda i, j: (i, g0 + j)),
                  pl.BlockSpec((tm, tn), lambda i, j: (i, g0 + gstep + j)),
                  pl.BlockSpec((tm, tn), lambda i, j: (i, g0 + 2 * gstep + j))],
        out_specs=pl.BlockSpec((tm, tn), lambda i, j: (i, j)),
        compiler_params=_params("parallel", "arbitrary"),
        name="merge",
    )(ya, yb, yc, w_branch, w_branch, w_branch, h, h, h)


def _lower_bound_kernel(x_ref, o_ref):
    x = x_ref[...].astype(F32)
    e = jnp.exp(x - jnp.max(x, axis=0, keepdims=True))
    sm = e / jnp.sum(e, axis=0, keepdims=True)
    row = lax.broadcasted_iota(jnp.int32, x.shape, 0)
    acc = jnp.zeros_like(x)
    for i in range(1, x.shape[0]):
        acc = acc + jnp.where(row >= i, sm[i:i + 1, :], 0.0)
    o_ref[...] = acc


def _lower_bounds(lb_logits):
    return pl.pallas_call(
        _lower_bound_kernel,
        out_shape=jax.ShapeDtypeStruct(lb_logits.shape, F32),
        name="hgrn_lower_bound",
    )(lb_logits)


def _toeplitz_kernel(w_ref, mask_ref, o_ref, *, mult):
    rows, cols = o_ref.shape[-2:]
    p = w_ref.shape[-1]
    x = jnp.broadcast_to(w_ref[0, 0], (rows, p))
    x = pltpu.roll(x, 0, 1, stride=1, stride_axis=0)
    o_ref[0, 0] = x[:, :cols] * mult + mask_ref[0]


def _rel_bias_blocks(rel_bias, rows, cols, deltas, masks):
    p = -(-(rows + cols - 1) // LANE) * LANE
    k = np.arange(p)
    diag = np.where(k < cols, -k, p - k)
    idx = np.stack([np.clip(diag + dl, -MAX_REL, MAX_REL) + MAX_REL for dl in deltas])
    n_tab = rel_bias.shape[0] * rel_bias.shape[1]
    w = jnp.take(rel_bias.astype(F32).reshape(n_tab, -1), jnp.asarray(idx.reshape(-1)), axis=1)
    w = w.reshape(n_tab, len(deltas), 1, p)
    return pl.pallas_call(
        functools.partial(_toeplitz_kernel, mult=DH_A ** 0.5),
        out_shape=jax.ShapeDtypeStruct((n_tab, len(deltas), rows, cols), F32),
        grid=(n_tab, len(deltas)),
        in_specs=[pl.BlockSpec((1, 1, 1, p), lambda i, j: (i, j, 0, 0)),
                  pl.BlockSpec((1, rows, cols), lambda i, j: (j, 0, 0))],
        out_specs=pl.BlockSpec((1, 1, rows, cols), lambda i, j: (i, j, 0, 0)),
        compiler_params=_params("parallel", "parallel"),
        name="rel_bias_blocks",
    )(w, jnp.asarray(masks, F32))


def _band_mask(q_pos, k_pos):
    qc, kc = q_pos // CHUNK, k_pos // CHUNK
    valid = (kc[None, :] <= qc[:, None]) & (kc[None, :] >= qc[:, None] - N_LEFT_CHUNKS)
    return np.where(valid, 0.0, NEG)


def _prompt_bias(rel_bias):
    r = np.arange(ATT_TQ)
    deltas = [(ATT_NKB - 1 - j) * ATT_TQ for j in range(ATT_NKB)]
    base = BAND
    masks = np.stack([_band_mask(base + r, base - dl + r) for dl in deltas])
    return _rel_bias_blocks(rel_bias, ATT_TQ, ATT_TQ, deltas, masks)


def _sample_bias(rel_bias, t, ca):
    q_pos = PAST_LEN + np.arange(t)
    k_pos = np.concatenate([PAST_LEN - ca + np.arange(ca), q_pos])
    mask = _band_mask(q_pos, k_pos) + np.where(k_pos >= 0, 0.0, NEG)[None, :]
    b = _rel_bias_blocks(rel_bias, t, ca + t, [ca], mask[None])[:, 0]
    return b[:, :, :ca], b[:, :, ca:]


def _softmax_pv(scores, values, c, ones_col):
    if all(s.shape == scores[0].shape for s in scores):
        mx = scores[0]
        for s in scores[1:]:
            mx = jnp.maximum(mx, s)
        m = jnp.max(mx, axis=-1, keepdims=True)
    else:
        m = jnp.max(scores[0], axis=-1, keepdims=True)
        for s in scores[1:]:
            m = jnp.maximum(m, jnp.max(s, axis=-1, keepdims=True))
    dh = values[0].shape[1]
    l = None
    o = None
    for s, v in zip(scores, values):
        p = jnp.exp2((s - m) * c).astype(BF16)
        if ones_col:
            v = jnp.concatenate([v, jnp.ones_like(v)], axis=1)
        else:
            ps = jnp.sum(p.astype(F32), axis=-1, keepdims=True)
            l = ps if l is None else l + ps
        pv = jnp.dot(p, v, preferred_element_type=F32)
        o = pv if o is None else o + pv
    if ones_col:
        return o[:, :dh] / o[:, dh:]
    return o / l


def _attn_prompt_kernel(q_ref, *refs, n_heads, dh, nkb, scale):
    k_refs = refs[:nkb]
    v_refs = refs[nkb:2 * nkb]
    z_ref, bias_ref, o_ref = refs[2 * nkb:]
    t = pl.program_id(1)
    for hd in range(n_heads):
        sl = slice(hd * dh, (hd + 1) * dh)
        q = q_ref[0, :, sl]
        scores = []
        for j in range(nkb):
            s = lax.dot_general(q, k_refs[j][0, :, sl], NT_DIMS, preferred_element_type=F32)
            s = s + bias_ref[hd, j]
            if j < nkb - 1:
                s = jnp.where(t >= nkb - 1 - j, s, NEG)
            scores.append(s)
        o = _softmax_pv(scores, [v_refs[j][0, :, sl] for j in range(nkb)], scale * LOG2E, True)
        z = z_ref[0, :, sl].astype(F32)
        o_ref[0, :, sl] = (o * _silu(z)).astype(o_ref.dtype)


def _attn_prompt(h3, bias, layer, d_a):
    bsz, t, _ = h3.shape
    n_heads = d_a // DH_A
    nkb = ATT_NKB
    assert t % ATT_TQ == 0 and ATT_TQ % CHUNK == 0 and BAND % ATT_TQ == 0

    def kv_spec(col, j):
        return pl.BlockSpec((1, ATT_TQ, d_a),
                            lambda b, i: (b, jnp.maximum(i - (nkb - 1 - j), 0), col))

    in_specs = ([pl.BlockSpec((1, ATT_TQ, d_a), lambda b, i: (b, i, 0))]
                + [kv_spec(1, j) for j in range(nkb)]
                + [kv_spec(2, j) for j in range(nkb)]
                + [pl.BlockSpec((1, ATT_TQ, d_a), lambda b, i: (b, i, 3)),
                   pl.BlockSpec((n_heads,) + bias.shape[1:], lambda b, i: (layer, 0, 0, 0))])
    return pl.pallas_call(
        functools.partial(_attn_prompt_kernel, n_heads=n_heads, dh=DH_A, nkb=nkb, scale=DH_A ** -0.5),
        out_shape=jax.ShapeDtypeStruct((bsz, t, d_a), BF16),
        grid=(bsz, t // ATT_TQ),
        in_specs=in_specs,
        out_specs=pl.BlockSpec((1, ATT_TQ, d_a), lambda b, i: (b, i, 0)),
        compiler_params=_params("parallel", "arbitrary"),
        name="attn_prompt",
    )(h3, *([h3] * (2 * nkb)), h3, bias)


def _attn_sample_kernel(q_ref, k_ref, v_ref, z_ref, kc_ref, vc_ref, bc_ref, bn_ref, o_ref, *, n_heads, dh, scale):
    for hd in range(n_heads):
        sl = slice(hd * dh, (hd + 1) * dh)
        q = q_ref[0, :, sl]
        s_c = lax.dot_general(q, kc_ref[0, hd].astype(BF16), NT_DIMS, preferred_element_type=F32)
        s_n = lax.dot_general(q, k_ref[0, :, sl], NT_DIMS, preferred_element_type=F32)
        scores = [s_c + bc_ref[hd], s_n + bn_ref[hd]]
        o = _softmax_pv(scores, [vc_ref[0, hd].astype(BF16), v_ref[0, :, sl]], scale * LOG2E, True)
        z = z_ref[0, :, sl].astype(F32)
        o_ref[0, :, sl] = (o * _silu(z)).astype(o_ref.dtype)


def _attn_sample(h3, k_cache, v_cache, bias_c, bias_n, layer, d_a):
    bsz, t, _ = h3.shape
    ca = k_cache.shape[3]
    n_heads = d_a // DH_A
    assert k_cache.shape[2:] == (n_heads, ca, DH_A)
    row = lambda col: pl.BlockSpec((1, t, d_a), lambda b: (b, 0, col))
    cache = pl.BlockSpec((None, 1, n_heads, ca, DH_A), lambda b: (layer, b, 0, 0, 0))
    bias = lambda a: pl.BlockSpec((n_heads,) + a.shape[1:], lambda b: (layer, 0, 0))
    return pl.pallas_call(
        functools.partial(_attn_sample_kernel, n_heads=n_heads, dh=DH_A, scale=DH_A ** -0.5),
        out_shape=jax.ShapeDtypeStruct((bsz, t, d_a), BF16),
        grid=(bsz,),
        in_specs=[row(0), row(1), row(2), row(3), cache, cache, bias(bias_c), bias(bias_n)],
        out_specs=pl.BlockSpec((1, t, d_a), lambda b: (b, 0, 0)),
        compiler_params=_params("parallel"),
        name="attn_sample",
    )(h3, h3, h3, h3, k_cache, v_cache, bias_c, bias_n)


def _mem_attn_kernel(q_ref, z_ref, mk_ref, mv_ref, o_ref, *, n_heads, dh, scale, head_axis):
    for hd in range(n_heads):
        sl = slice(hd * dh, (hd + 1) * dh)
        mk = mk_ref[0, :, hd, :] if head_axis else mk_ref[0, :, sl]
        mv = mv_ref[0, :, hd, :] if head_axis else mv_ref[0, :, sl]
        s = lax.dot_general(q_ref[0, :, sl], mk.astype(BF16), NT_DIMS, preferred_element_type=F32)
        o = _softmax_pv([s], [mv.astype(BF16)], scale * LOG2E, False)
        z = z_ref[0, :, sl].astype(F32)
        o_ref[0, :, sl] = (o * _silu(z)).astype(o_ref.dtype)


def _mem_attn(h3, mk, mv, mk_spec, mv_spec, head_axis, d_c, q_col0, tq):
    bsz, t, _ = h3.shape
    tq = min(tq, t)
    assert q_col0 % d_c == 0
    qb = q_col0 // d_c
    dh = d_c // H_C
    return pl.pallas_call(
        functools.partial(_mem_attn_kernel, n_heads=H_C, dh=dh, scale=dh ** -0.5, head_axis=head_axis),
        out_shape=jax.ShapeDtypeStruct((bsz, t, d_c), BF16),
        grid=(bsz, t // tq),
        in_specs=[pl.BlockSpec((1, tq, d_c), lambda b, i: (b, i, qb)),
                  pl.BlockSpec((1, tq, d_c), lambda b, i: (b, i, qb + 1)),
                  mk_spec, mv_spec],
        out_specs=pl.BlockSpec((1, tq, d_c), lambda b, i: (b, i, 0)),
        compiler_params=_params("parallel", "arbitrary"),
        name="mem_attn",
    )(h3, h3, mk, mv)


def _seg_cumsum(x, row, group):
    d = 1
    while d < group:
        x = x + jnp.where((row & (group - 1)) >= d, pltpu.roll(x, d, 0), 0.0)
        d *= 2
    return x


def _hgrn_masks():
    t = np.arange(HG_L)[:, None]
    s = np.arange(HG_L)[None, :]
    m16 = [(t // 16 == i) & (s < 16 * i) for i in range(1, 4)]
    m4 = [(t // 16 == s // 16) & ((t % 16) // 4 == i) & (s % 16 < 4 * i) for i in range(1, 4)]
    base = (t // 4 == s // 4) & (s <= t)
    m1 = np.concatenate(m16, axis=1).astype(np.float32)
    m2 = np.concatenate(m4 + [base], axis=1).astype(np.float32)
    return jnp.asarray(m1), jnp.asarray(m2)


def _bcast_rows(ref, offsets, reps):
    return jnp.concatenate(
        [jnp.broadcast_to(ref[o:o + 1, :], (reps, LANE)) for o in offsets], axis=0)


def _hgrn_blocks(q, g, k, v, sts, chain_head, m1, m2, cs_ref):
    L = HG_L
    n = q.shape[0] // L
    starts = [c * L for c in range(n)]
    row = lax.broadcasted_iota(jnp.int32, q.shape, 0)
    r64 = row & (L - 1)
    c4 = _seg_cumsum(g, row, 4)
    c16 = _seg_cumsum(g, row, 16)
    c16_ref, b_ref = cs_ref.at[0], cs_ref.at[1]
    c16_ref[...] = c16
    b = c16
    for j in range(1, L // 16):
        b = b + jnp.where(r64 >= 16 * j, _bcast_rows(c16_ref, [s + 16 * j - 1 for s in starts], L), 0.0)
    b_ref[...] = b

    q4 = (q * jnp.exp2(c4)).astype(BF16)
    q16 = (q * jnp.exp2(c16)).astype(BF16)
    q64 = (q * jnp.exp2(b)).astype(BF16)

    k16 = []
    for i in range(1, L // 16):
        ref_rows = _bcast_rows(b_ref, [s + 16 * i - 1 for s in starts], L)
        k16.append((k * jnp.exp2(jnp.minimum(ref_rows - b, 0.0))).astype(BF16))
    k4 = []
    for i in range(1, 4):
        ref_rows = _bcast_rows(c16_ref, [s + 16 * j + 4 * i - 1 for s in starts for j in range(L // 16)], 16)
        k4.append((k * jnp.exp2(jnp.minimum(ref_rows - c16, 0.0))).astype(BF16))
    k4.append((k * jnp.exp2(-c4)).astype(BF16))
    kd = (k * jnp.exp2(_bcast_rows(b_ref, [s + L - 1 for s in starts], L) - b)).astype(BF16)
    vb = v.astype(BF16)

    scores = []
    for s in starts:
        sl = slice(s, s + L)
        a1 = lax.dot_general(q16[sl], jnp.concatenate([x[sl] for x in k16], axis=0), NT_DIMS,
                             preferred_element_type=F32) * m1
        a2 = lax.dot_general(q4[sl], jnp.concatenate([x[sl] for x in k4], axis=0), NT_DIMS,
                             preferred_element_type=F32) * m2
        scores.append((a1.astype(BF16), a2.astype(BF16)))
    sts = list(sts)
    outs = []
    for c, s in enumerate(starts):
        sl = slice(s, s + L)
        a1, a2 = scores[c]
        st = sts[chain_head[c]]
        vc = vb[sl]
        outs.append(
            jnp.dot(a1, jnp.concatenate([vc] * 3, axis=0), preferred_element_type=F32)
            + jnp.dot(a2, jnp.concatenate([vc] * 4, axis=0), preferred_element_type=F32)
            + lax.dot_general(q64[sl], st.astype(BF16), NT_DIMS, preferred_element_type=F32))
        sts[chain_head[c]] = (st * jnp.exp2(b_ref[s + L - 1:s + L, :])
                              + lax.dot_general(vc, kd[sl], TN_DIMS, preferred_element_type=F32))
    return jnp.concatenate(outs, axis=0), sts


def _hgrn_kernel(q_ref, f_ref, i_ref, z_ref, lb_ref, gn_ref, s0_ref, m1_ref, m2_ref,
                 y_ref, s_out_ref, st_ref, cs_ref, *, rows, n_sub, unroll, groups):
    tb = pl.program_id(2)

    @pl.when(tb == 0)
    def _():
        for p in range(HG_HP):
            st_ref[p] = s0_ref[0, p].astype(F32).T

    m1 = m1_ref[...]
    m2 = m2_ref[...]
    chains = [(u, p) for u in range(unroll) for p in range(HG_HP)]
    chain_head = [p for _, p in chains]
    lb = jnp.concatenate([jnp.broadcast_to(lb_ref[p], (HG_L, LANE)) for _, p in chains], axis=0)
    gn = jnp.concatenate([jnp.broadcast_to(gn_ref[p], (HG_L, LANE)) for _, p in chains], axis=0)

    def stack(ref, r0s):
        parts = []
        for u, p in chains:
            x = ref[0, pl.ds(r0s[u], rows), p * LANE:(p + 1) * LANE].astype(F32)
            if rows < HG_L:
                x = jnp.concatenate([x, jnp.zeros((HG_L - rows, LANE), F32)], axis=0)
            parts.append(x)
        return jnp.concatenate(parts, axis=0)

    def group(blk0, sts, scratch):
        r0s = [pl.multiple_of((blk0 + u) * rows, rows) for u in range(unroll)]
        qr, fr, vr, zr = (stack(ref, r0s) for ref in (q_ref, f_ref, i_ref, z_ref))
        kf = (1.0 - lb) * jax.nn.sigmoid(-fr)
        g = jnp.log1p(-kf) * LOG2E
        if rows < HG_L:
            live = (lax.broadcasted_iota(jnp.int32, fr.shape, 0) & (HG_L - 1)) < rows
            kf = jnp.where(live, kf, 0.0)
            g = jnp.where(live, g, 0.0)
        o, sts = _hgrn_blocks(_silu(qr), g, kf, vr, sts, chain_head, m1, m2, scratch)
        on = o * lax.rsqrt(jnp.mean(o * o, axis=-1, keepdims=True) + EPS) * gn
        y = (on * _silu(zr)).astype(y_ref.dtype)
        for c, (u, p) in enumerate(chains):
            y_ref[0, pl.ds(r0s[u], rows), p * LANE:(p + 1) * LANE] = y[c * HG_L:c * HG_L + rows]
        return sts

    def body(it, sts):
        for gi in range(groups):
            sts = group((it * groups + gi) * unroll, sts, cs_ref.at[gi])
        return tuple(sts)

    sts = lax.fori_loop(0, n_sub // (unroll * groups), body, tuple(st_ref[p] for p in range(HG_HP)))
    for p in range(HG_HP):
        st_ref[p] = sts[p]

    @pl.when(tb == pl.num_programs(2) - 1)
    def _():
        for p in range(HG_HP):
            s_out_ref[0, p] = sts[p].T


def _hgrn(h3, lb, gain, s0, layer, masks, col0, d_b):
    bsz, t, _ = h3.shape
    n_heads = d_b // DK_B
    assert col0 % (HG_HP * LANE) == 0 and n_heads % HG_HP == 0 and DK_B == LANE
    assert s0.shape[1:] == (bsz, n_heads, DK_B, LANE)
    if t % HG_L == 0:
        rows, tb = HG_L, min(HG_TB, t)
    else:
        assert t < HG_L and t % 16 == 0
        rows, tb = t, t
    n_sub = tb // rows
    unroll = HG_UNROLL if n_sub % HG_UNROLL == 0 else 1
    groups = HG_GROUPS if n_sub % (unroll * HG_GROUPS) == 0 else 1
    assert t % tb == 0
    w = HG_HP * LANE
    c0 = col0 // w
    col = lambda seg: pl.BlockSpec((1, tb, w), lambda b, hp, i: (b, i, c0 + seg * (n_heads // HG_HP) + hp))
    per_head = pl.BlockSpec((HG_HP, 1, LANE), lambda b, hp, i: (hp, 0, 0))
    m1, m2 = masks
    y, s_new = pl.pallas_call(
        functools.partial(_hgrn_kernel, rows=rows, n_sub=n_sub, unroll=unroll, groups=groups),
        out_shape=(jax.ShapeDtypeStruct((bsz, t, d_b), BF16),
                   jax.ShapeDtypeStruct(s0.shape[1:], F32)),
        grid=(bsz, n_heads // HG_HP, t // tb),
        in_specs=[col(0), col(1), col(2), col(3), per_head, per_head,
                  pl.BlockSpec((None, 1, HG_HP, DK_B, LANE), lambda b, hp, i: (layer, b, hp, 0, 0)),
                  pl.BlockSpec(m1.shape, lambda b, hp, i: (0, 0)),
                  pl.BlockSpec(m2.shape, lambda b, hp, i: (0, 0))],
        out_specs=(pl.BlockSpec((1, tb, w), lambda b, hp, i: (b, i, hp)),
                   pl.BlockSpec((1, HG_HP, DK_B, LANE), lambda b, hp, i: (b, hp, 0, 0))),
        scratch_shapes=[pltpu.VMEM((HG_HP, LANE, DK_B), F32),
                        pltpu.VMEM((groups, 2, unroll * HG_HP * HG_L, LANE), F32)],
        compiler_params=_params("parallel", "parallel", "arbitrary"),
        name="hgrn2",
    )(h3, h3, h3, h3, lb.reshape(n_heads, 1, DK_B), gain.reshape(n_heads, 1, LANE).astype(F32),
      s0, m1, m2)
    return y, s_new


def _layer(x, attend, s0, s0_layer, mem_args, norm_g, w_in, lb, hgrn_g, w_branch, w_out, masks, dims, casts=()):
    d_a, d_b, d_c = dims
    bsz, t, d = x.shape
    m = bsz * t
    x2 = x.reshape(m, d)
    xn = _rmsnorm(x2, norm_g, BF16, 256)
    cast_out = ()
    if casts:
        h, *cast_out = _matmul(xn, w_in, 0, BF16, 1024, 1024, casts=casts)
        w_branch, w_out = cast_out[:2]
    else:
        h = _matmul(xn, w_in, 0, BF16, 1024, 1024)
    h3 = h.reshape(bsz, t, -1)
    ya = attend(h3)
    yb, s_new = _hgrn(h3, lb, hgrn_g, s0, s0_layer, masks, 4 * d_a, d_b)
    yc = _mem_attn(h3, *mem_args, d_c, 4 * d_a + 4 * d_b, 512)
    merged = _merge(ya.reshape(m, d_a), yb.reshape(m, d_b), yc.reshape(m, d_c), h, w_branch, 0,
                    4 * d_a + 4 * d_b + 2 * d_c, 1024, 1024)
    x_new = _matmul(merged, w_out, 0, F32, 1024, 1024, res=x2).reshape(bsz, t, d)
    ka = h3[:, :, d_a:2 * d_a]
    va = h3[:, :, 2 * d_a:3 * d_a]
    return x_new, ka, va, s_new, tuple(cast_out)


def kernel(x_prompt, x_sample, mem_prompt, cache_attn_k, cache_attn_v, state_hgrn, cache_mem_k, cache_mem_v, norm_gain, w_in, rel_bias, lb_logits, hgrn_norm_gain, mem_norm_gain, w_mem_kv, w_branch, w_out, final_norm_gain):
    depth = w_in.shape[0]
    bp, tp, d = x_prompt.shape
    bs, ts, _ = x_sample.shape
    h_a = rel_bias.shape[1]
    d_a = h_a * DH_A
    d_b = lb_logits.shape[1]
    d_c = w_mem_kv.shape[2] // 2
    h_b = d_b // DK_B
    n_mem = mem_prompt.shape[1]
    dims = (d_a, d_b, d_c)
    ca_s = cache_attn_k.shape[2]
    ca_p = min(BAND, tp)

    w_in_l = w_in[:1].astype(BF16)
    w_mem_b = w_mem_kv.astype(BF16)
    cast_cols = 256

    kc_s = jnp.swapaxes(cache_attn_k, 2, 3)
    vc_s = jnp.swapaxes(cache_attn_v, 2, 3)

    lb_all = _lower_bounds(lb_logits)
    masks = _hgrn_masks()
    bias_p = _prompt_bias(rel_bias)
    bias_c, bias_n = _sample_bias(rel_bias, ts, ca_s)
    s0_p = jnp.zeros((1, bp, h_b, DK_B, d_b // h_b), F32)
    mem2 = mem_prompt.reshape(bp * n_mem, d)

    xp, xs = x_prompt, x_sample
    kp_l, vp_l, sp_l, mkp_l, mvp_l, ks_l, vs_l, ss_l = [], [], [], [], [], [], [], []
    for l in range(depth):
        casts = [(w_branch, l, cast_cols), (w_out, l, cast_cols)]
        if l + 1 < depth:
            casts.append((w_in, l + 1, 1024))

        memn = _rmsnorm(mem2, mem_norm_gain[l], BF16, 256)
        mkv = _matmul(memn, w_mem_b, l, F32, 1024, 1024).reshape(bp, n_mem, 2 * d_c)
        mem_p = (mkv, mkv,
                 pl.BlockSpec((1, n_mem, d_c), lambda b, i: (b, 0, 0)),
                 pl.BlockSpec((1, n_mem, d_c), lambda b, i: (b, 0, 1)), False)
        att_p = lambda h3: _attn_prompt(h3, bias_p, l, d_a)
        xp, ka, va, s_fin, cast_out = _layer(xp, att_p, s0_p, 0, mem_p, norm_gain[l], w_in_l, lb_all[l],
                                             hgrn_norm_gain[l], None, None, masks, dims, casts=casts)
        w_branch_l, w_out_l = cast_out[:2]
        kp_l.append(ka[:, -ca_p:].astype(F32).reshape(bp, ca_p, h_a, DH_A))
        vp_l.append(va[:, -ca_p:].astype(F32).reshape(bp, ca_p, h_a, DH_A))
        sp_l.append(s_fin)
        mkp_l.append(mkv[:, :, :d_c].reshape(bp, n_mem, H_C, d_c // H_C))
        mvp_l.append(mkv[:, :, d_c:].reshape(bp, n_mem, H_C, d_c // H_C))

        mem_blk = pl.BlockSpec((None, 1, n_mem, H_C, d_c // H_C), lambda b, i: (l, b, 0, 0, 0))
        mem_s = (cache_mem_k, cache_mem_v, mem_blk, mem_blk, True)
        att_s = lambda h3: _attn_sample(h3, kc_s, vc_s, bias_c, bias_n, l, d_a)
        xs, ka_s, va_s, s_new, _ = _layer(xs, att_s, state_hgrn, l, mem_s, norm_gain[l], w_in_l, lb_all[l],
                                          hgrn_norm_gain[l], w_branch_l, w_out_l, masks, dims)
        if l + 1 < depth:
            w_in_l = cast_out[2]
        ks_l.append(ka_s.astype(F32).reshape(bs, ts, h_a, DH_A))
        vs_l.append(va_s.astype(F32).reshape(bs, ts, h_a, DH_A))
        ss_l.append(s_new)

    y_prompt = _rmsnorm(xp.reshape(bp * tp, d), final_norm_gain, F32, 256).reshape(bp, tp, d)
    y_sample = _rmsnorm(xs.reshape(bs * ts, d), final_norm_gain, F32, 256).reshape(bs, ts, d)
    return (y_prompt, y_sample,
            jnp.stack(kp_l), jnp.stack(vp_l), jnp.stack(sp_l), jnp.stack(mkp_l), jnp.stack(mvp_l),
            jnp.stack(ks_l), jnp.stack(vs_l), jnp.stack(ss_l))
```

```python
import functools

import numpy as np
import jax
import jax.numpy as jnp
from jax import lax
from jax.experimental import pallas as pl
from jax.experimental.pallas import tpu as pltpu

F32 = jnp.float32
BF16 = jnp.bfloat16

PAST_LEN = 4096
CHUNK = 64
N_LEFT_CHUNKS = 8
BAND = N_LEFT_CHUNKS * CHUNK
DH_A = 128
MAX_REL = 256
DK_B = 128
H_C = 4
EPS = 1e-6
NEG = -1e30
LOG2E = 1.4426950408889634

LANE = 128
VMEM_LIMIT_BYTES = 56 * 1024 * 1024

ATT_TQ = 256
ATT_NKB = BAND // ATT_TQ + 1
HG_L = 64
HG_TB = 512
HG_HP = 4
HG_UNROLL = 2

NT_DIMS = (((1,), (1,)), ((), ()))
TN_DIMS = (((0,), (0,)), ((), ()))


def _params(*sem, flags=None):
    return pltpu.CompilerParams(dimension_semantics=sem, vmem_limit_bytes=VMEM_LIMIT_BYTES, flags=flags)


def _silu(z):
    return z * jax.nn.sigmoid(z)


def _rmsnorm_kernel(x_ref, g_ref, o_ref):
    x = x_ref[...].astype(F32)
    y = x * lax.rsqrt(jnp.mean(x * x, axis=-1, keepdims=True) + EPS)
    o_ref[...] = (y * g_ref[...]).astype(o_ref.dtype)


def _rmsnorm(x, g, out_dtype, tm):
    m, d = x.shape
    tm = min(tm, m)
    return pl.pallas_call(
        _rmsnorm_kernel,
        out_shape=jax.ShapeDtypeStruct((m, d), out_dtype),
        grid=(m // tm,),
        in_specs=[pl.BlockSpec((tm, d), lambda i: (i, 0)),
                  pl.BlockSpec((1, d), lambda i: (0, 0))],
        out_specs=pl.BlockSpec((tm, d), lambda i: (i, 0)),
        compiler_params=_params("parallel"),
        name="rmsnorm",
    )(x, g.reshape(1, d).astype(F32))


def _mm_kernel(a_ref, b_ref, o_ref):
    o_ref[...] = jnp.dot(a_ref[...], b_ref[...], preferred_element_type=F32).astype(o_ref.dtype)


def _mm_res_kernel(a_ref, b_ref, r_ref, o_ref):
    o_ref[...] = r_ref[...] + jnp.dot(a_ref[...], b_ref[...], preferred_element_type=F32)


def _mm_cast_kernel(a_ref, b_ref, *refs):
    n_cast = (len(refs) - 1) // 2
    src_refs, o_ref, dst_refs = refs[:n_cast], refs[n_cast], refs[n_cast + 1:]
    o_ref[...] = jnp.dot(a_ref[...], b_ref[...], preferred_element_type=F32).astype(o_ref.dtype)
    for src, dst in zip(src_refs, dst_refs):
        dst[...] = src[...].astype(dst.dtype)


def _matmul(a, w, layer, out_dtype, tm, tn, res=None, casts=()):
    m, k = a.shape
    n = w.shape[2]
    tm, tn = min(tm, m), min(tn, n)
    gi, gj = m // tm, n // tn
    in_specs = [pl.BlockSpec((tm, k), lambda i, j: (i, 0)),
                pl.BlockSpec((None, k, tn), lambda i, j: (layer, 0, j))]
    args = [a, w]
    out_shape = [jax.ShapeDtypeStruct((m, n), out_dtype)]
    out_specs = [pl.BlockSpec((tm, tn), lambda i, j: (i, j))]
    kern, name = _mm_kernel, "matmul"
    if res is not None:
        assert not casts
        in_specs.append(pl.BlockSpec((tm, tn), lambda i, j: (i, j)))
        args.append(res)
        kern, name = _mm_res_kernel, "matmul_res"
    if casts:
        kern, name = _mm_cast_kernel, "matmul_cast"
    for src, src_layer, bc in casts:
        _, r, c = src.shape
        br, nj = r // gi, c // bc
        assert br * gi == r and nj * bc == c and nj <= gj
        in_specs.append(pl.BlockSpec((None, br, bc),
                                     lambda i, j, sl=src_layer, nj=nj: (sl, i, jnp.minimum(j, nj - 1))))
        args.append(src)
        out_shape.append(jax.ShapeDtypeStruct((1, r, c), BF16))
        out_specs.append(pl.BlockSpec((None, br, bc), lambda i, j, nj=nj: (0, i, jnp.minimum(j, nj - 1))))
    outs = pl.pallas_call(
        kern,
        out_shape=out_shape,
        grid=(gi, gj),
        in_specs=in_specs,
        out_specs=out_specs,
        compiler_params=_params("parallel", "arbitrary"),
        name=name,
    )(*args)
    return outs if casts else outs[0]


def _norm_mm_cast_kernel(x_ref, g_ref, b_ref, *refs, gi, nxb):
    n_cast = (len(refs) - 2) // 2
    src_refs, o_ref, dst_refs, xn_ref = refs[:n_cast], refs[n_cast], refs[n_cast + 1:-1], refs[-1]
    i, j = pl.program_id(0), pl.program_id(1)
    rb = x_ref.shape[0]

    def norm_block():
        x = x_ref[...].astype(F32)
        y = x * lax.rsqrt(jnp.mean(x * x, axis=-1, keepdims=True) + EPS)
        r0 = pl.multiple_of(jnp.minimum(j, nxb - 1) * rb, rb)
        xn_ref[i % 2, pl.ds(r0, rb), :] = (y * g_ref[...]).astype(xn_ref.dtype)

    def matmul_tile():
        o_ref[...] = jnp.dot(xn_ref[(i + 1) % 2], b_ref[...], preferred_element_type=F32).astype(o_ref.dtype)
        for src, dst in zip(src_refs, dst_refs):
            dst[...] = src[...].astype(dst.dtype)

    @pl.when(i == 0)
    def _():
        norm_block()

    @pl.when(jnp.logical_and(i > 0, i < gi))
    def _():
        matmul_tile()
        norm_block()

    @pl.when(i == gi)
    def _():
        matmul_tile()


def _norm_matmul(x, g, w, out_dtype, tm, tn, rb, casts=()):
    m, k = x.shape
    n = w.shape[2]
    gi, gj = m // tm, n // tn
    nxb = tm // rb
    assert gi * tm == m and gj * tn == n and nxb * rb == tm and nxb <= gj
    live = lambda i, v: jnp.where(i > 0, v, 0)
    in_specs = [pl.BlockSpec((rb, k), lambda i, j: (jnp.minimum(i, gi - 1) * nxb + jnp.minimum(j, nxb - 1), 0)),
                pl.BlockSpec((1, k), lambda i, j: (0, 0)),
                pl.BlockSpec((None, k, tn), lambda i, j: (0, 0, live(i, j)))]
    args = [x, g.reshape(1, k).astype(F32), w]
    out_shape = [jax.ShapeDtypeStruct((m, n), out_dtype)]
    out_specs = [pl.BlockSpec((tm, tn), lambda i, j: (jnp.maximum(i - 1, 0), live(i, j)))]
    for src, src_layer, bc in casts:
        _, r, c = src.shape
        br, nj = r // gi, c // bc
        assert br * gi == r and nj * bc == c and nj <= gj
        in_specs.append(pl.BlockSpec(
            (None, br, bc),
            lambda i, j, sl=src_layer, nj=nj: (sl, jnp.maximum(i - 1, 0), live(i, jnp.minimum(j, nj - 1)))))
        args.append(src)
        out_shape.append(jax.ShapeDtypeStruct((1, r, c), BF16))
        out_specs.append(pl.BlockSpec(
            (None, br, bc), lambda i, j, nj=nj: (0, jnp.maximum(i - 1, 0), live(i, jnp.minimum(j, nj - 1)))))
    return pl.pallas_call(
        functools.partial(_norm_mm_cast_kernel, gi=gi, nxb=nxb),
        out_shape=out_shape,
        grid=(gi + 1, gj),
        in_specs=in_specs,
        out_specs=out_specs,
        scratch_shapes=[pltpu.VMEM((2, tm, k), BF16)],
        compiler_params=_params("arbitrary", "arbitrary"),
        name="norm_matmul_cast",
    )(*args)


def _merge_kernel(ya_ref, yb_ref, yc_ref, wa_ref, wb_ref, wc_ref, ga_ref, gb_ref, gc_ref, o_ref):
    def part(y_ref, w_ref, g_ref):
        gate = jax.nn.sigmoid(g_ref[...].astype(F32))
        return gate * jnp.dot(y_ref[...], w_ref[...], preferred_element_type=F32)

    o_ref[...] = (part(ya_ref, wa_ref, ga_ref) + part(yb_ref, wb_ref, gb_ref)
                  + part(yc_ref, wc_ref, gc_ref)).astype(o_ref.dtype)


def _merge(ya, yb, yc, h, w_branch, layer, gate_col0, tm, tn):
    m, d_a = ya.shape
    d_b, d_c = yb.shape[1], yc.shape[1]
    d = w_branch.shape[2]
    tm = min(tm, m)
    assert d_a == d_b and (d_a + d_b) % d_c == 0 and gate_col0 % tn == 0 and d % tn == 0
    g0, gstep = gate_col0 // tn, d // tn
    return pl.pallas_call(
        _merge_kernel,
        out_shape=jax.ShapeDtypeStruct((m, d), BF16),
        grid=(m // tm, d // tn),
        in_specs=[pl.BlockSpec((tm, d_a), lambda i, j: (i, 0)),
                  pl.BlockSpec((tm, d_b), lambda i, j: (i, 0)),
                  pl.BlockSpec((tm, d_c), lambda i, j: (i, 0)),
                  pl.BlockSpec((None, d_a, tn), lambda i, j: (layer, 0, j)),
                  pl.BlockSpec((None, d_b, tn), lambda i, j: (layer, 1, j)),
                  pl.BlockSpec((None, d_c, tn), lambda i, j: (layer, (d_a + d_b) // d_c, j)),
                  pl.BlockSpec((tm, tn), lambda i, j: (i, g0 + j)),
                  pl.BlockSpec((tm, tn), lambda i, j: (i, g0 + gstep + j)),
                  pl.BlockSpec((tm, tn), lambda i, j: (i, g0 + 2 * gstep + j))],
        out_specs=pl.BlockSpec((tm, tn), lambda i, j: (i, j)),
        compiler_params=_params("parallel", "arbitrary"),
        name="merge",
    )(ya, yb, yc, w_branch, w_branch, w_branch, h, h, h)


def _lower_bound_kernel(x_ref, o_ref):
    x = x_ref[...].astype(F32)
    e = jnp.exp(x - jnp.max(x, axis=0, keepdims=True))
    sm = e / jnp.sum(e, axis=0, keepdims=True)
    row = lax.broadcasted_iota(jnp.int32, x.shape, 0)
    acc = jnp.zeros_like(x)
    for i in range(1, x.shape[0]):
        acc = acc + jnp.where(row >= i, sm[i:i + 1, :], 0.0)
    o_ref[...] = acc


def _lower_bounds(lb_logits):
    return pl.pallas_call(
        _lower_bound_kernel,
        out_shape=jax.ShapeDtypeStruct(lb_logits.shape, F32),
        name="hgrn_lower_bound",
    )(lb_logits)


def _toeplitz_kernel(w_ref, mask_ref, o_ref, *, mult):
    rows, cols = o_ref.shape[-2:]
    p = w_ref.shape[-1]
    x = jnp.broadcast_to(w_ref[0, 0], (rows, p))
    x = pltpu.roll(x, 0, 1, stride=1, stride_axis=0)
    o_ref[0, 0] = x[:, :cols] * mult + mask_ref[0]


def _rel_bias_blocks(rel_bias, rows, cols, deltas, masks):
    p = -(-(rows + cols - 1) // LANE) * LANE
    k = np.arange(p)
    diag = np.where(k < cols, -k, p - k)
    idx = np.stack([np.clip(diag + dl, -MAX_REL, MAX_REL) + MAX_REL for dl in deltas])
    n_tab = rel_bias.shape[0] * rel_bias.shape[1]
    w = jnp.take(rel_bias.astype(F32).reshape(n_tab, -1), jnp.asarray(idx.reshape(-1)), axis=1)
    w = w.reshape(n_tab, len(deltas), 1, p)
    return pl.pallas_call(
        functools.partial(_toeplitz_kernel, mult=DH_A ** 0.5),
        out_shape=jax.ShapeDtypeStruct((n_tab, len(deltas), rows, cols), F32),
        grid=(n_tab, len(deltas)),
        in_specs=[pl.BlockSpec((1, 1, 1, p), lambda i, j: (i, j, 0, 0)),
                  pl.BlockSpec((1, rows, cols), lambda i, j: (j, 0, 0))],
        out_specs=pl.BlockSpec((1, 1, rows, cols), lambda i, j: (i, j, 0, 0)),
        compiler_params=_params("parallel", "parallel"),
        name="rel_bias_blocks",
    )(w, jnp.asarray(masks, F32))


def _band_mask(q_pos, k_pos):
    qc, kc = q_pos // CHUNK, k_pos // CHUNK
    valid = (kc[None, :] <= qc[:, None]) & (kc[None, :] >= qc[:, None] - N_LEFT_CHUNKS)
    return np.where(valid, 0.0, NEG)


def _prompt_bias(rel_bias):
    r = np.arange(ATT_TQ)
    deltas = [(ATT_NKB - 1 - j) * ATT_TQ for j in range(ATT_NKB)]
    base = BAND
    masks = np.stack([_band_mask(base + r, base - dl + r) for dl in deltas])
    return _rel_bias_blocks(rel_bias, ATT_TQ, ATT_TQ, deltas, masks)


def _sample_bias(rel_bias, t, ca):
    q_pos = PAST_LEN + np.arange(t)
    k_pos = np.concatenate([PAST_LEN - ca + np.arange(ca), q_pos])
    mask = _band_mask(q_pos, k_pos) + np.where(k_pos >= 0, 0.0, NEG)[None, :]
    b = _rel_bias_blocks(rel_bias, t, ca + t, [ca], mask[None])[:, 0]
    return b[:, :, :ca], b[:, :, ca:]


def _softmax_pv(scores, values, c, ones_col):
    if all(s.shape == scores[0].shape for s in scores):
        mx = scores[0]
        for s in scores[1:]:
            mx = jnp.maximum(mx, s)
        m = jnp.max(mx, axis=-1, keepdims=True)
    else:
        m = jnp.max(scores[0], axis=-1, keepdims=True)
        for s in scores[1:]:
            m = jnp.maximum(m, jnp.max(s, axis=-1, keepdims=True))
    dh = values[0].shape[1]
    l = None
    o = None
    for s, v in zip(scores, values):
        p = jnp.exp2((s - m) * c).astype(BF16)
        if ones_col:
            v = jnp.concatenate([v, jnp.ones_like(v)], axis=1)
        else:
            ps = jnp.sum(p.astype(F32), axis=-1, keepdims=True)
            l = ps if l is None else l + ps
        pv = jnp.dot(p, v, preferred_element_type=F32)
        o = pv if o is None else o + pv
    if ones_col:
        return o[:, :dh] / o[:, dh:]
    return o / l


def _attn_prompt_kernel(q_ref, *refs, n_heads, dh, nkb, scale):
    k_refs = refs[:nkb]
    v_refs = refs[nkb:2 * nkb]
    z_ref, bias_ref, o_ref = refs[2 * nkb:]
    t = pl.program_id(1)

    def head_scores(hd):
        sl = slice(hd * dh, (hd + 1) * dh)
        q = q_ref[0, :, sl]
        scores = []
        for j in range(nkb):
            s = lax.dot_general(q, k_refs[j][0, :, sl], NT_DIMS, preferred_element_type=F32)
            s = s + bias_ref[hd, j]
            if j < nkb - 1:
                s = jnp.where(t >= nkb - 1 - j, s, NEG)
            scores.append(s)
        return scores

    scores = head_scores(0)
    for hd in range(n_heads):
        nxt = head_scores(hd + 1) if hd + 1 < n_heads else None
        sl = slice(hd * dh, (hd + 1) * dh)
        o = _softmax_pv(scores, [v_refs[j][0, :, sl] for j in range(nkb)], scale * LOG2E, True)
        z = z_ref[0, :, sl].astype(F32)
        o_ref[0, :, sl] = (o * _silu(z)).astype(o_ref.dtype)
        scores = nxt


def _attn_prompt(h3, bias, layer, d_a):
    bsz, t, _ = h3.shape
    n_heads = d_a // DH_A
    nkb = ATT_NKB
    assert t % ATT_TQ == 0 and ATT_TQ % CHUNK == 0 and BAND % ATT_TQ == 0

    def kv_spec(col, j):
        return pl.BlockSpec((1, ATT_TQ, d_a),
                            lambda b, i: (b, jnp.maximum(i - (nkb - 1 - j), 0), col))

    in_specs = ([pl.BlockSpec((1, ATT_TQ, d_a), lambda b, i: (b, i, 0))]
                + [kv_spec(1, j) for j in range(nkb)]
                + [kv_spec(2, j) for j in range(nkb)]
                + [pl.BlockSpec((1, ATT_TQ, d_a), lambda b, i: (b, i, 3)),
                   pl.BlockSpec((n_heads,) + bias.shape[1:], lambda b, i: (layer, 0, 0, 0))])
    return pl.pallas_call(
        functools.partial(_attn_prompt_kernel, n_heads=n_heads, dh=DH_A, nkb=nkb, scale=DH_A ** -0.5),
        out_shape=jax.ShapeDtypeStruct((bsz, t, d_a), BF16),
        grid=(bsz, t // ATT_TQ),
        in_specs=in_specs,
        out_specs=pl.BlockSpec((1, ATT_TQ, d_a), lambda b, i: (b, i, 0)),
        compiler_params=_params("parallel", "arbitrary"),
        name="attn_prompt",
    )(h3, *([h3] * (2 * nkb)), h3, bias)


def _attn_sample_kernel(q_ref, k_ref, v_ref, z_ref, kc_ref, vc_ref, bc_ref, bn_ref, o_ref, *, n_heads, dh, scale):
    for hd in range(n_heads):
        sl = slice(hd * dh, (hd + 1) * dh)
        q = q_ref[0, :, sl]
        s_c = lax.dot_general(q, kc_ref[0, hd].astype(BF16), NT_DIMS, preferred_element_type=F32)
        s_n = lax.dot_general(q, k_ref[0, :, sl], NT_DIMS, preferred_element_type=F32)
        scores = [s_c + bc_ref[hd], s_n + bn_ref[hd]]
        o = _softmax_pv(scores, [vc_ref[0, hd].astype(BF16), v_ref[0, :, sl]], scale * LOG2E, True)
        z = z_ref[0, :, sl].astype(F32)
        o_ref[0, :, sl] = (o * _silu(z)).astype(o_ref.dtype)


def _attn_sample(h3, k_cache, v_cache, bias_c, bias_n, layer, d_a):
    bsz, t, _ = h3.shape
    ca = k_cache.shape[3]
    n_heads = d_a // DH_A
    assert k_cache.shape[2:] == (n_heads, ca, DH_A)
    row = lambda col: pl.BlockSpec((1, t, d_a), lambda b: (b, 0, col))
    cache = pl.BlockSpec((None, 1, n_heads, ca, DH_A), lambda b: (layer, b, 0, 0, 0))
    bias = lambda a: pl.BlockSpec((n_heads,) + a.shape[1:], lambda b: (layer, 0, 0))
    return pl.pallas_call(
        functools.partial(_attn_sample_kernel, n_heads=n_heads, dh=DH_A, scale=DH_A ** -0.5),
        out_shape=jax.ShapeDtypeStruct((bsz, t, d_a), BF16),
        grid=(bsz,),
        in_specs=[row(0), row(1), row(2), row(3), cache, cache, bias(bias_c), bias(bias_n)],
        out_specs=pl.BlockSpec((1, t, d_a), lambda b: (b, 0, 0)),
        compiler_params=_params("parallel"),
        name="attn_sample",
    )(h3, h3, h3, h3, k_cache, v_cache, bias_c, bias_n)


def _mem_attn_kernel(q_ref, z_ref, mk_ref, mv_ref, o_ref, *, n_heads, dh, scale, head_axis):
    for hd in range(n_heads):
        sl = slice(hd * dh, (hd + 1) * dh)
        mk = mk_ref[0, :, hd, :] if head_axis else mk_ref[0, :, sl]
        mv = mv_ref[0, :, hd, :] if head_axis else mv_ref[0, :, sl]
        s = lax.dot_general(q_ref[0, :, sl], mk.astype(BF16), NT_DIMS, preferred_element_type=F32)
        o = _softmax_pv([s], [mv.astype(BF16)], scale * LOG2E, False)
        z = z_ref[0, :, sl].astype(F32)
        o_ref[0, :, sl] = (o * _silu(z)).astype(o_ref.dtype)


def _mem_attn(h3, mk, mv, mk_spec, mv_spec, head_axis, d_c, q_col0, tq):
    bsz, t, _ = h3.shape
    tq = min(tq, t)
    assert q_col0 % d_c == 0
    qb = q_col0 // d_c
    dh = d_c // H_C
    return pl.pallas_call(
        functools.partial(_mem_attn_kernel, n_heads=H_C, dh=dh, scale=dh ** -0.5, head_axis=head_axis),
        out_shape=jax.ShapeDtypeStruct((bsz, t, d_c), BF16),
        grid=(bsz, t // tq),
        in_specs=[pl.BlockSpec((1, tq, d_c), lambda b, i: (b, i, qb)),
                  pl.BlockSpec((1, tq, d_c), lambda b, i: (b, i, qb + 1)),
                  mk_spec, mv_spec],
        out_specs=pl.BlockSpec((1, tq, d_c), lambda b, i: (b, i, 0)),
        compiler_params=_params("parallel", "arbitrary"),
        name="mem_attn",
    )(h3, h3, mk, mv)


def _seg_cumsum(x, row, group):
    d = 1
    while d < group:
        x = x + jnp.where((row & (group - 1)) >= d, pltpu.roll(x, d, 0), 0.0)
        d *= 2
    return x


def _hgrn_masks():
    t = np.arange(HG_L)[:, None]
    s = np.arange(HG_L)[None, :]
    m16 = [(t // 16 == i) & (s < 16 * i) for i in range(1, 4)]
    m4 = [(t // 16 == s // 16) & ((t % 16) // 4 == i) & (s % 16 < 4 * i) for i in range(1, 4)]
    base = (t // 4 == s // 4) & (s <= t)
    m1 = np.concatenate(m16, axis=1).astype(np.float32)
    m2 = np.concatenate(m4 + [base], axis=1).astype(np.float32)
    return jnp.asarray(m1), jnp.asarray(m2)


def _bcast_rows(ref, offsets, reps):
    return jnp.concatenate(
        [jnp.broadcast_to(ref[o:o + 1, :], (reps, LANE)) for o in offsets], axis=0)


def _hgrn_prep(q, g, k, v, cs_ref):
    L = HG_L
    n = q.shape[0] // L
    starts = [c * L for c in range(n)]
    row = lax.broadcasted_iota(jnp.int32, q.shape, 0)
    r64 = row & (L - 1)
    c4 = _seg_cumsum(g, row, 4)
    c16 = _seg_cumsum(g, row, 16)
    c16_ref, b_ref = cs_ref.at[0], cs_ref.at[1]
    c16_ref[...] = c16
    b = c16
    for j in range(1, L // 16):
        b = b + jnp.where(r64 >= 16 * j, _bcast_rows(c16_ref, [s + 16 * j - 1 for s in starts], L), 0.0)
    b_ref[...] = b

    k16 = []
    for i in range(1, L // 16):
        ref_rows = _bcast_rows(b_ref, [s + 16 * i - 1 for s in starts], L)
        k16.append((k * jnp.exp2(jnp.minimum(ref_rows - b, 0.0))).astype(BF16))
    k4 = []
    for i in range(1, 4):
        ref_rows = _bcast_rows(c16_ref, [s + 16 * j + 4 * i - 1 for s in starts for j in range(L // 16)], 16)
        k4.append((k * jnp.exp2(jnp.minimum(ref_rows - c16, 0.0))).astype(BF16))
    k4.append((k * jnp.exp2(-c4)).astype(BF16))
    return dict(
        starts=starts, b_ref=b_ref, k16=k16, k4=k4, vb=v.astype(BF16),
        q4=(q * jnp.exp2(c4)).astype(BF16), q16=(q * jnp.exp2(c16)).astype(BF16),
        q64=(q * jnp.exp2(b)).astype(BF16),
        kd=(k * jnp.exp2(_bcast_rows(b_ref, [s + L - 1 for s in starts], L) - b)).astype(BF16))


def _hgrn_scores(p, m1, m2):
    scores = []
    for s in p["starts"]:
        sl = slice(s, s + HG_L)
        a1 = lax.dot_general(p["q16"][sl], jnp.concatenate([x[sl] for x in p["k16"]], axis=0), NT_DIMS,
                             preferred_element_type=F32) * m1
        a2 = lax.dot_general(p["q4"][sl], jnp.concatenate([x[sl] for x in p["k4"]], axis=0), NT_DIMS,
                             preferred_element_type=F32) * m2
        scores.append((a1.astype(BF16), a2.astype(BF16)))
    return scores


def _hgrn_outputs(p, scores, sts, chain_head):
    L = HG_L
    sts = list(sts)
    outs = []
    for c, s in enumerate(p["starts"]):
        sl = slice(s, s + L)
        a1, a2 = scores[c]
        st = sts[chain_head[c]]
        vc = p["vb"][sl]
        outs.append(
            jnp.dot(a1, jnp.concatenate([vc] * 3, axis=0), preferred_element_type=F32)
            + jnp.dot(a2, jnp.concatenate([vc] * 4, axis=0), preferred_element_type=F32)
            + lax.dot_general(p["q64"][sl], st.astype(BF16), NT_DIMS, preferred_element_type=F32))
        sts[chain_head[c]] = (st * jnp.exp2(p["b_ref"][s + L - 1:s + L, :])
                              + lax.dot_general(vc, p["kd"][sl], TN_DIMS, preferred_element_type=F32))
    return jnp.concatenate(outs, axis=0), sts


def _hgrn_unit(q_ref, f_ref, i_ref, z_ref, lb_ref, gn_ref, m1_ref, m2_ref, y_ref, st_ref, cs_ref,
               *, rows, n_sub, unroll):
    m1 = m1_ref[...]
    m2 = m2_ref[...]
    chains = [(u, p) for u in range(unroll) for p in range(HG_HP)]
    chain_head = [p for _, p in chains]
    lb = jnp.concatenate([jnp.broadcast_to(lb_ref[p], (HG_L, LANE)) for _, p in chains], axis=0)
    gn = jnp.concatenate([jnp.broadcast_to(gn_ref[p], (HG_L, LANE)) for _, p in chains], axis=0)

    def stack(ref, r0s):
        parts = []
        for u, p in chains:
            x = ref[0, r0s[u]:r0s[u] + rows, p * LANE:(p + 1) * LANE].astype(F32)
            if rows < HG_L:
                x = jnp.concatenate([x, jnp.zeros((HG_L - rows, LANE), F32)], axis=0)
            parts.append(x)
        return jnp.concatenate(parts, axis=0)

    def finish(pending, sts):
        p, scores, zr, r0s = pending
        o, sts = _hgrn_outputs(p, scores, sts, chain_head)
        on = o * lax.rsqrt(jnp.mean(o * o, axis=-1, keepdims=True) + EPS) * gn
        y = (on * _silu(zr)).astype(y_ref.dtype)
        for c, (u, hp) in enumerate(chains):
            y_ref[0, r0s[u]:r0s[u] + rows, hp * LANE:(hp + 1) * LANE] = y[c * HG_L:c * HG_L + rows]
        return sts

    sts = [st_ref[p] for p in range(HG_HP)]
    pending = None
    for gi in range(n_sub // unroll):
        r0s = [(gi * unroll + u) * rows for u in range(unroll)]
        qr, fr, vr, zr = (stack(ref, r0s) for ref in (q_ref, f_ref, i_ref, z_ref))
        kf = (1.0 - lb) * jax.nn.sigmoid(-fr)
        g = jnp.log1p(-kf) * LOG2E
        if rows < HG_L:
            live = (lax.broadcasted_iota(jnp.int32, fr.shape, 0) & (HG_L - 1)) < rows
            kf = jnp.where(live, kf, 0.0)
            g = jnp.where(live, g, 0.0)
        p = _hgrn_prep(_silu(qr), g, kf, vr, cs_ref.at[gi])
        yield
        scores = _hgrn_scores(p, m1, m2)
        if pending is not None:
            sts = finish(pending, sts)
        pending = (p, scores, zr, r0s)
        yield
    sts = finish(pending, sts)
    for p in range(HG_HP):
        st_ref[p] = sts[p]


def _hgrn_kernel(q_ref, f_ref, i_ref, z_ref, lb_ref, gn_ref, s0_ref, m1_ref, m2_ref,
                 y_ref, s_out_ref, st_ref, cs_ref, *, rows, n_sub, unroll):
    tb = pl.program_id(2)

    @pl.when(tb == 0)
    def _():
        for p in range(HG_HP):
            st_ref[p] = s0_ref[0, p].astype(F32).T

    for _ in _hgrn_unit(q_ref, f_ref, i_ref, z_ref, lb_ref, gn_ref, m1_ref, m2_ref, y_ref, st_ref, cs_ref,
                        rows=rows, n_sub=n_sub, unroll=unroll):
        pass

    @pl.when(tb == pl.num_programs(2) - 1)
    def _():
        for p in range(HG_HP):
            s_out_ref[0, p] = st_ref[p].T


def _hgrn(h3, lb, gain, s0, layer, masks, col0, d_b):
    bsz, t, _ = h3.shape
    n_heads = d_b // DK_B
    assert col0 % (HG_HP * LANE) == 0 and n_heads % HG_HP == 0 and DK_B == LANE
    assert s0.shape[1:] == (bsz, n_heads, DK_B, LANE)
    if t % HG_L == 0:
        rows, tb = HG_L, min(HG_TB, t)
    else:
        assert t < HG_L and t % 16 == 0
        rows, tb = t, t
    n_sub = tb // rows
    unroll = HG_UNROLL if n_sub % HG_UNROLL == 0 else 1
    assert t % tb == 0
    w = HG_HP * LANE
    c0 = col0 // w
    col = lambda seg: pl.BlockSpec((1, tb, w), lambda b, hp, i: (b, i, c0 + seg * (n_heads // HG_HP) + hp))
    per_head = pl.BlockSpec((HG_HP, 1, LANE), lambda b, hp, i: (hp, 0, 0))
    m1, m2 = masks
    y, s_new = pl.pallas_call(
        functools.partial(_hgrn_kernel, rows=rows, n_sub=n_sub, unroll=unroll),
        out_shape=(jax.ShapeDtypeStruct((bsz, t, d_b), BF16),
                   jax.ShapeDtypeStruct(s0.shape[1:], F32)),
        grid=(bsz, n_heads // HG_HP, t // tb),
        in_specs=[col(0), col(1), col(2), col(3), per_head, per_head,
                  pl.BlockSpec((None, 1, HG_HP, DK_B, LANE), lambda b, hp, i: (layer, b, hp, 0, 0)),
                  pl.BlockSpec(m1.shape, lambda b, hp, i: (0, 0)),
                  pl.BlockSpec(m2.shape, lambda b, hp, i: (0, 0))],
        out_specs=(pl.BlockSpec((1, tb, w), lambda b, hp, i: (b, i, hp)),
                   pl.BlockSpec((1, HG_HP, DK_B, LANE), lambda b, hp, i: (b, hp, 0, 0))),
        scratch_shapes=[pltpu.VMEM((HG_HP, LANE, DK_B), F32),
                        pltpu.VMEM((n_sub // unroll, 2, unroll * HG_HP * HG_L, LANE), F32)],
        compiler_params=_params("parallel", "parallel", "arbitrary"),
        name="hgrn2",
    )(h3, h3, h3, h3, lb.reshape(n_heads, 1, DK_B), gain.reshape(n_heads, 1, LANE).astype(F32),
      s0, m1, m2)
    return y, s_new


def _layer(x, attend, s0, s0_layer, mem_args, norm_g, w_in, lb, hgrn_g, w_branch, w_out, masks, dims, casts=()):
    d_a, d_b, d_c = dims
    bsz, t, d = x.shape
    m = bsz * t
    x2 = x.reshape(m, d)
    cast_out = ()
    if casts:
        h, *cast_out = _norm_matmul(x2, norm_g, w_in, BF16, 1024, 1024, 64, casts=casts)
        w_branch, w_out = cast_out[:2]
    else:
        h = _matmul(_rmsnorm(x2, norm_g, BF16, 256), w_in, 0, BF16, 1024, 1024)
    h3 = h.reshape(bsz, t, -1)
    ya = attend(h3)
    yb, s_new = _hgrn(h3, lb, hgrn_g, s0, s0_layer, masks, 4 * d_a, d_b)
    yc = _mem_attn(h3, *mem_args, d_c, 4 * d_a + 4 * d_b, 512)
    merged = _merge(ya.reshape(m, d_a), yb.reshape(m, d_b), yc.reshape(m, d_c), h, w_branch, 0,
                    4 * d_a + 4 * d_b + 2 * d_c, 1024, 1024)
    x_new = _matmul(merged, w_out, 0, F32, 1024, 1024, res=x2).reshape(bsz, t, d)
    ka = h3[:, :, d_a:2 * d_a]
    va = h3[:, :, 2 * d_a:3 * d_a]
    return x_new, ka, va, s_new, tuple(cast_out)


def kernel(x_prompt, x_sample, mem_prompt, cache_attn_k, cache_attn_v, state_hgrn, cache_mem_k, cache_mem_v, norm_gain, w_in, rel_bias, lb_logits, hgrn_norm_gain, mem_norm_gain, w_mem_kv, w_branch, w_out, final_norm_gain):
    depth = w_in.shape[0]
    bp, tp, d = x_prompt.shape
    bs, ts, _ = x_sample.shape
    h_a = rel_bias.shape[1]
    d_a = h_a * DH_A
    d_b = lb_logits.shape[1]
    d_c = w_mem_kv.shape[2] // 2
    h_b = d_b // DK_B
    n_mem = mem_prompt.shape[1]
    dims = (d_a, d_b, d_c)
    ca_s = cache_attn_k.shape[2]
    ca_p = min(BAND, tp)

    w_in_l = w_in[:1].astype(BF16)
    w_mem_b = w_mem_kv.astype(BF16)
    cast_cols = 256

    kc_s = jnp.swapaxes(cache_attn_k, 2, 3)
    vc_s = jnp.swapaxes(cache_attn_v, 2, 3)

    lb_all = _lower_bounds(lb_logits)
    masks = _hgrn_masks()
    bias_p = _prompt_bias(rel_bias)
    bias_c, bias_n = _sample_bias(rel_bias, ts, ca_s)
    s0_p = jnp.zeros((1, bp, h_b, DK_B, d_b // h_b), F32)
    mem2 = mem_prompt.reshape(bp * n_mem, d)

    xp, xs = x_prompt, x_sample
    kp_l, vp_l, sp_l, mkp_l, mvp_l, ks_l, vs_l, ss_l = [], [], [], [], [], [], [], []
    for l in range(depth):
        casts = [(w_branch, l, cast_cols), (w_out, l, cast_cols)]
        if l + 1 < depth:
            casts.append((w_in, l + 1, 1024))

        memn = _rmsnorm(mem2, mem_norm_gain[l], BF16, 256)
        mkv = _matmul(memn, w_mem_b, l, F32, 1024, 1024).reshape(bp, n_mem, 2 * d_c)
        mem_p = (mkv, mkv,
                 pl.BlockSpec((1, n_mem, d_c), lambda b, i: (b, 0, 0)),
                 pl.BlockSpec((1, n_mem, d_c), lambda b, i: (b, 0, 1)), False)
        att_p = lambda h3: _attn_prompt(h3, bias_p, l, d_a)
        xp, ka, va, s_fin, cast_out = _layer(xp, att_p, s0_p, 0, mem_p, norm_gain[l], w_in_l, lb_all[l],
                                             hgrn_norm_gain[l], None, None, masks, dims, casts=casts)
        w_branch_l, w_out_l = cast_out[:2]
        kp_l.append(ka[:, -ca_p:].astype(F32).reshape(bp, ca_p, h_a, DH_A))
        vp_l.append(va[:, -ca_p:].astype(F32).reshape(bp, ca_p, h_a, DH_A))
        sp_l.append(s_fin)
        mkp_l.append(mkv[:, :, :d_c].reshape(bp, n_mem, H_C, d_c // H_C))
        mvp_l.append(mkv[:, :, d_c:].reshape(bp, n_mem, H_C, d_c // H_C))

        mem_blk = pl.BlockSpec((None, 1, n_mem, H_C, d_c // H_C), lambda b, i: (l, b, 0, 0, 0))
        mem_s = (cache_mem_k, cache_mem_v, mem_blk, mem_blk, True)
        att_s = lambda h3: _attn_sample(h3, kc_s, vc_s, bias_c, bias_n, l, d_a)
        xs, ka_s, va_s, s_new, _ = _layer(xs, att_s, state_hgrn, l, mem_s, norm_gain[l], w_in_l, lb_all[l],
                                          hgrn_norm_gain[l], w_branch_l, w_out_l, masks, dims)
        if l + 1 < depth:
            w_in_l = cast_out[2]
        ks_l.append(ka_s.astype(F32).reshape(bs, ts, h_a, DH_A))
        vs_l.append(va_s.astype(F32).reshape(bs, ts, h_a, DH_A))
        ss_l.append(s_new)

    y_prompt = _rmsnorm(xp.reshape(bp * tp, d), final_norm_gain, F32, 256).reshape(bp, tp, d)
    y_sample = _rmsnorm(xs.reshape(bs * ts, d), final_norm_gain, F32, 256).reshape(bs, ts, d)
    return (y_prompt, y_sample,
            jnp.stack(kp_l), jnp.stack(vp_l), jnp.stack(sp_l), jnp.stack(mkp_l), jnp.stack(mvp_l),
            jnp.stack(ks_l), jnp.stack(vs_l), jnp.stack(ss_l))
```

```python
import functools

import numpy as np
import jax
import jax.numpy as jnp
from jax import lax
from jax.experimental import pallas as pl
from jax.experimental.pallas import tpu as pltpu

F32 = jnp.float32
BF16 = jnp.bfloat16

PAST_LEN = 4096
CHUNK = 64
N_LEFT_CHUNKS = 8
BAND = N_LEFT_CHUNKS * CHUNK
DH_A = 128
MAX_REL = 256
DK_B = 128
H_C = 4
EPS = 1e-6
NEG = -1e30
LOG2E = 1.4426950408889634

LANE = 128
VMEM_LIMIT_BYTES = 56 * 1024 * 1024

ATT_TQ = 256
ATT_NKB = BAND // ATT_TQ + 1
HG_L = 64
HG_TB = 2048
HG_HP = 4
HG_UNROLL = 4

NT_DIMS = (((1,), (1,)), ((), ()))
TN_DIMS = (((0,), (0,)), ((), ()))


def _params(*sem):
    return pltpu.CompilerParams(dimension_semantics=sem, vmem_limit_bytes=VMEM_LIMIT_BYTES)


def _sigmoid(x):
    return 0.5 * jnp.tanh(0.5 * x) + 0.5


def _silu(z):
    return z * _sigmoid(z)


def _rmsnorm_kernel(x_ref, g_ref, o_ref):
    x = x_ref[...].astype(F32)
    y = x * lax.rsqrt(jnp.mean(x * x, axis=-1, keepdims=True) + EPS)
    o_ref[...] = (y * g_ref[...]).astype(o_ref.dtype)


def _rmsnorm(x, g, out_dtype, tm):
    m, d = x.shape
    tm = min(tm, m)
    return pl.pallas_call(
        _rmsnorm_kernel,
        out_shape=jax.ShapeDtypeStruct((m, d), out_dtype),
        grid=(m // tm,),
        in_specs=[pl.BlockSpec((tm, d), lambda i: (i, 0)),
                  pl.BlockSpec((1, d), lambda i: (0, 0))],
        out_specs=pl.BlockSpec((tm, d), lambda i: (i, 0)),
        compiler_params=_params("parallel"),
        name="rmsnorm",
    )(x, g.reshape(1, d).astype(F32))


def _mm_kernel(a_ref, b_ref, o_ref):
    o_ref[...] = jnp.dot(a_ref[...], b_ref[...], preferred_element_type=F32).astype(o_ref.dtype)


def _mm_res_kernel(a_ref, b_ref, r_ref, o_ref):
    o_ref[...] = r_ref[...] + jnp.dot(a_ref[...], b_ref[...], preferred_element_type=F32)


def _mm_cast_kernel(a_ref, b_ref, *refs):
    n_cast = (len(refs) - 1) // 2
    src_refs, o_ref, dst_refs = refs[:n_cast], refs[n_cast], refs[n_cast + 1:]
    o_ref[...] = jnp.dot(a_ref[...], b_ref[...], preferred_element_type=F32).astype(o_ref.dtype)
    for src, dst in zip(src_refs, dst_refs):
        dst[...] = src[...].astype(dst.dtype)


def _matmul(a, w, layer, out_dtype, tm, tn, res=None, casts=()):
    m, k = a.shape
    n = w.shape[2]
    tm, tn = min(tm, m), min(tn, n)
    gi, gj = m // tm, n // tn
    in_specs = [pl.BlockSpec((tm, k), lambda i, j: (i, 0)),
                pl.BlockSpec((None, k, tn), lambda i, j: (layer, 0, j))]
    args = [a, w]
    out_shape = [jax.ShapeDtypeStruct((m, n), out_dtype)]
    out_specs = [pl.BlockSpec((tm, tn), lambda i, j: (i, j))]
    kern, name = _mm_kernel, "matmul"
    if res is not None:
        assert not casts
        in_specs.append(pl.BlockSpec((tm, tn), lambda i, j: (i, j)))
        args.append(res)
        kern, name = _mm_res_kernel, "matmul_res"
    if casts:
        kern, name = _mm_cast_kernel, "matmul_cast"
    for src, src_layer, bc in casts:
        _, r, c = src.shape
        br, nj = r // gi, c // bc
        assert br * gi == r and nj * bc == c and nj <= gj
        in_specs.append(pl.BlockSpec((None, br, bc),
                                     lambda i, j, sl=src_layer, nj=nj: (sl, i, jnp.minimum(j, nj - 1))))
        args.append(src)
        out_shape.append(jax.ShapeDtypeStruct((1, r, c), BF16))
        out_specs.append(pl.BlockSpec((None, br, bc), lambda i, j, nj=nj: (0, i, jnp.minimum(j, nj - 1))))
    outs = pl.pallas_call(
        kern,
        out_shape=out_shape,
        grid=(gi, gj),
        in_specs=in_specs,
        out_specs=out_specs,
        compiler_params=_params("parallel", "arbitrary"),
        name=name,
    )(*args)
    return outs if casts else outs[0]


def _norm_mm_cast_kernel(x_ref, g_ref, b_ref, *refs, gi, nxb):
    n_cast = (len(refs) - 2) // 2
    src_refs, o_ref, dst_refs, xn_ref = refs[:n_cast], refs[n_cast], refs[n_cast + 1:-1], refs[-1]
    i, j = pl.program_id(0), pl.program_id(1)
    rb = x_ref.shape[0]

    def norm_block():
        x = x_ref[...].astype(F32)
        y = x * lax.rsqrt(jnp.mean(x * x, axis=-1, keepdims=True) + EPS)
        r0 = pl.multiple_of(jnp.minimum(j, nxb - 1) * rb, rb)
        xn_ref[i % 2, pl.ds(r0, rb), :] = (y * g_ref[...]).astype(xn_ref.dtype)

    def matmul_tile():
        o_ref[...] = jnp.dot(xn_ref[(i + 1) % 2], b_ref[...], preferred_element_type=F32).astype(o_ref.dtype)
        for src, dst in zip(src_refs, dst_refs):
            dst[...] = src[...].astype(dst.dtype)

    @pl.when(i == 0)
    def _():
        norm_block()

    @pl.when(jnp.logical_and(i > 0, i < gi))
    def _():
        matmul_tile()
        norm_block()

    @pl.when(i == gi)
    def _():
        matmul_tile()


def _norm_matmul(x, g, w, out_dtype, tm, tn, rb, casts=()):
    m, k = x.shape
    n = w.shape[2]
    gi, gj = m // tm, n // tn
    nxb = tm // rb
    assert gi * tm == m and gj * tn == n and nxb * rb == tm and nxb <= gj
    live = lambda i, v: jnp.where(i > 0, v, 0)
    in_specs = [pl.BlockSpec((rb, k), lambda i, j: (jnp.minimum(i, gi - 1) * nxb + jnp.minimum(j, nxb - 1), 0)),
                pl.BlockSpec((1, k), lambda i, j: (0, 0)),
                pl.BlockSpec((None, k, tn), lambda i, j: (0, 0, live(i, j)))]
    args = [x, g.reshape(1, k).astype(F32), w]
    out_shape = [jax.ShapeDtypeStruct((m, n), out_dtype)]
    out_specs = [pl.BlockSpec((tm, tn), lambda i, j: (jnp.maximum(i - 1, 0), live(i, j)))]
    for src, src_layer, bc in casts:
        _, r, c = src.shape
        br, nj = r // gi, c // bc
        assert br * gi == r and nj * bc == c and nj <= gj
        in_specs.append(pl.BlockSpec(
            (None, br, bc),
            lambda i, j, sl=src_layer, nj=nj: (sl, jnp.maximum(i - 1, 0), live(i, jnp.minimum(j, nj - 1)))))
        args.append(src)
        out_shape.append(jax.ShapeDtypeStruct((1, r, c), BF16))
        out_specs.append(pl.BlockSpec(
            (None, br, bc), lambda i, j, nj=nj: (0, jnp.maximum(i - 1, 0), live(i, jnp.minimum(j, nj - 1)))))
    return pl.pallas_call(
        functools.partial(_norm_mm_cast_kernel, gi=gi, nxb=nxb),
        out_shape=out_shape,
        grid=(gi + 1, gj),
        in_specs=in_specs,
        out_specs=out_specs,
        scratch_shapes=[pltpu.VMEM((2, tm, k), BF16)],
        compiler_params=_params("arbitrary", "arbitrary"),
        name="norm_matmul_cast",
    )(*args)


def _merge_kernel(ya_ref, yb_ref, yc_ref, wa_ref, wb_ref, wc_ref, ga_ref, gb_ref, gc_ref, o_ref):
    def part(y_ref, w_ref, g_ref):
        gate = _sigmoid(g_ref[...].astype(F32))
        return gate * jnp.dot(y_ref[...], w_ref[...], preferred_element_type=F32)

    o_ref[...] = (part(ya_ref, wa_ref, ga_ref) + part(yb_ref, wb_ref, gb_ref)
                  + part(yc_ref, wc_ref, gc_ref)).astype(o_ref.dtype)


def _merge(ya, yb, yc, h, w_branch, layer, gate_col0, tm, tn):
    m, d_a = ya.shape
    d_b, d_c = yb.shape[1], yc.shape[1]
    d = w_branch.shape[2]
    tm = min(tm, m)
    assert d_a == d_b and (d_a + d_b) % d_c == 0 and gate_col0 % tn == 0 and d % tn == 0
    g0, gstep = gate_col0 // tn, d // tn
    return pl.pallas_call(
        _merge_kernel,
        out_shape=jax.ShapeDtypeStruct((m, d), BF16),
        grid=(m // tm, d // tn),
        in_specs=[pl.BlockSpec((tm, d_a), lambda i, j: (i, 0)),
                  pl.BlockSpec((tm, d_b), lambda i, j: (i, 0)),
                  pl.BlockSpec((tm, d_c), lambda i, j: (i, 0)),
                  pl.BlockSpec((None, d_a, tn), lambda i, j: (layer, 0, j)),
                  pl.BlockSpec((None, d_b, tn), lambda i, j: (layer, 1, j)),
                  pl.BlockSpec((None, d_c, tn), lambda i, j: (layer, (d_a + d_b) // d_c, j)),
                  pl.BlockSpec((tm, tn), lambda i, j: (i, g0 + j)),
                  pl.BlockSpec((tm, tn), lambda i, j: (i, g0 + gstep + j)),
                  pl.BlockSpec((tm, tn), lambda i, j: (i, g0 + 2 * gstep + j))],
        out_specs=pl.BlockSpec((tm, tn), lambda i, j: (i, j)),
        compiler_params=_params("parallel", "arbitrary"),
        name="merge",
    )(ya, yb, yc, w_branch, w_branch, w_branch, h, h, h)


def _lower_bound_kernel(x_ref, o_ref):
    x = x_ref[...].astype(F32)
    e = jnp.exp(x - jnp.max(x, axis=0, keepdims=True))
    sm = e / jnp.sum(e, axis=0, keepdims=True)
    row = lax.broadcasted_iota(jnp.int32, x.shape, 0)
    acc = jnp.zeros_like(x)
    for i in range(1, x.shape[0]):
        acc = acc + jnp.where(row >= i, sm[i:i + 1, :], 0.0)
    o_ref[...] = acc


def _lower_bounds(lb_logits):
    return pl.pallas_call(
        _lower_bound_kernel,
        out_shape=jax.ShapeDtypeStruct(lb_logits.shape, F32),
        name="hgrn_lower_bound",
    )(lb_logits)


def _toeplitz_kernel(w_ref, mask_ref, o_ref, *, mult):
    rows, cols = o_ref.shape[-2:]
    p = w_ref.shape[-1]
    for j in range(o_ref.shape[1]):
        x = jnp.broadcast_to(w_ref[0, j], (rows, p))
        x = pltpu.roll(x, 0, 1, stride=1, stride_axis=0)
        o_ref[0, j] = x[:, :cols] * mult + mask_ref[j]


def _rel_bias_blocks(rel_bias, rows, cols, deltas, masks):
    p = -(-(rows + cols - 1) // LANE) * LANE
    k = np.arange(p)
    diag = np.where(k < cols, -k, p - k)
    idx = np.stack([np.clip(diag + dl, -MAX_REL, MAX_REL) + MAX_REL for dl in deltas])
    n_tab = rel_bias.shape[0] * rel_bias.shape[1]
    w = jnp.take(rel_bias.astype(F32).reshape(n_tab, -1), jnp.asarray(idx.reshape(-1)), axis=1)
    w = w.reshape(n_tab, len(deltas), 1, p)
    return pl.pallas_call(
        functools.partial(_toeplitz_kernel, mult=DH_A ** 0.5),
        out_shape=jax.ShapeDtypeStruct((n_tab, len(deltas), rows, cols), F32),
        grid=(n_tab,),
        in_specs=[pl.BlockSpec((1, len(deltas), 1, p), lambda i: (i, 0, 0, 0)),
                  pl.BlockSpec((len(deltas), rows, cols), lambda i: (0, 0, 0))],
        out_specs=pl.BlockSpec((1, len(deltas), rows, cols), lambda i: (i, 0, 0, 0)),
        compiler_params=_params("parallel"),
        name="rel_bias_blocks",
    )(w, jnp.asarray(masks, F32))


def _band_mask(q_pos, k_pos):
    qc, kc = q_pos // CHUNK, k_pos // CHUNK
    valid = (kc[None, :] <= qc[:, None]) & (kc[None, :] >= qc[:, None] - N_LEFT_CHUNKS)
    return np.where(valid, 0.0, NEG)


def _prompt_bias(rel_bias):
    r = np.arange(ATT_TQ)
    deltas = [(ATT_NKB - 1 - j) * ATT_TQ for j in range(ATT_NKB)]
    base = BAND
    masks = np.stack([_band_mask(base + r, base - dl + r) for dl in deltas])
    return _rel_bias_blocks(rel_bias, ATT_TQ, ATT_TQ, deltas, masks)


def _sample_bias(rel_bias, t, ca):
    q_pos = PAST_LEN + np.arange(t)
    k_pos = np.concatenate([PAST_LEN - ca + np.arange(ca), q_pos])
    mask = _band_mask(q_pos, k_pos) + np.where(k_pos >= 0, 0.0, NEG)[None, :]
    b = _rel_bias_blocks(rel_bias, t, ca + t, [ca], mask[None])[:, 0]
    return b[:, :, :ca], b[:, :, ca:]


def _softmax_pv(scores, values, c, ones_col):
    if all(s.shape == scores[0].shape for s in scores):
        mx = scores[0]
        for s in scores[1:]:
            mx = jnp.maximum(mx, s)
        m = jnp.max(mx, axis=-1, keepdims=True)
    else:
        m = jnp.max(scores[0], axis=-1, keepdims=True)
        for s in scores[1:]:
            m = jnp.maximum(m, jnp.max(s, axis=-1, keepdims=True))
    dh = values[0].shape[1]
    l = None
    o = None
    for s, v in zip(scores, values):
        p = jnp.exp2((s - m) * c).astype(BF16)
        if ones_col:
            v = jnp.concatenate([v, jnp.ones_like(v)], axis=1)
        else:
            ps = jnp.sum(p.astype(F32), axis=-1, keepdims=True)
            l = ps if l is None else l + ps
        pv = jnp.dot(p, v, preferred_element_type=F32)
        o = pv if o is None else o + pv
    if ones_col:
        return o[:, :dh] / o[:, dh:]
    return o / l


def _attn_prompt_kernel(q_ref, *refs, n_heads, dh, nkb, scale):
    k_refs = refs[:nkb]
    v_refs = refs[nkb:2 * nkb]
    z_ref, bias_ref, o_ref = refs[2 * nkb:]
    t = pl.program_id(1)

    def head_scores(hd):
        sl = slice(hd * dh, (hd + 1) * dh)
        q = q_ref[0, :, sl]
        scores = []
        for j in range(nkb):
            s = lax.dot_general(q, k_refs[j][0, :, sl], NT_DIMS, preferred_element_type=F32)
            s = s + bias_ref[hd, j]
            if j < nkb - 1:
                s = jnp.where(t >= nkb - 1 - j, s, NEG)
            scores.append(s)
        return scores

    scores = head_scores(0)
    for hd in range(n_heads):
        nxt = head_scores(hd + 1) if hd + 1 < n_heads else None
        sl = slice(hd * dh, (hd + 1) * dh)
        o = _softmax_pv(scores, [v_refs[j][0, :, sl] for j in range(nkb)], scale * LOG2E, True)
        z = z_ref[0, :, sl].astype(F32)
        o_ref[0, :, sl] = (o * _silu(z)).astype(o_ref.dtype)
        scores = nxt


def _attn_prompt(h3, bias, layer, d_a):
    bsz, t, _ = h3.shape
    n_heads = d_a // DH_A
    nkb = ATT_NKB
    assert t % ATT_TQ == 0 and ATT_TQ % CHUNK == 0 and BAND % ATT_TQ == 0

    def kv_spec(col, j):
        return pl.BlockSpec((1, ATT_TQ, d_a),
                            lambda b, i: (b, jnp.maximum(i - (nkb - 1 - j), 0), col))

    in_specs = ([pl.BlockSpec((1, ATT_TQ, d_a), lambda b, i: (b, i, 0))]
                + [kv_spec(1, j) for j in range(nkb)]
                + [kv_spec(2, j) for j in range(nkb)]
                + [pl.BlockSpec((1, ATT_TQ, d_a), lambda b, i: (b, i, 3)),
                   pl.BlockSpec((n_heads,) + bias.shape[1:], lambda b, i: (layer, 0, 0, 0))])
    return pl.pallas_call(
        functools.partial(_attn_prompt_kernel, n_heads=n_heads, dh=DH_A, nkb=nkb, scale=DH_A ** -0.5),
        out_shape=jax.ShapeDtypeStruct((bsz, t, d_a), BF16),
        grid=(bsz, t // ATT_TQ),
        in_specs=in_specs,
        out_specs=pl.BlockSpec((1, ATT_TQ, d_a), lambda b, i: (b, i, 0)),
        compiler_params=_params("parallel", "arbitrary"),
        name="attn_prompt",
    )(h3, *([h3] * (2 * nkb)), h3, bias)


def _attn_sample_kernel(q_ref, k_ref, v_ref, z_ref, kc_ref, vc_ref, bc_ref, bn_ref, o_ref, *, n_heads, dh, scale):
    for hd in range(n_heads):
        sl = slice(hd * dh, (hd + 1) * dh)
        q = q_ref[0, :, sl]
        s_c = lax.dot_general(q, kc_ref[0, hd].astype(BF16), NT_DIMS, preferred_element_type=F32)
        s_n = lax.dot_general(q, k_ref[0, :, sl], NT_DIMS, preferred_element_type=F32)
        scores = [s_c + bc_ref[hd], s_n + bn_ref[hd]]
        o = _softmax_pv(scores, [vc_ref[0, hd].astype(BF16), v_ref[0, :, sl]], scale * LOG2E, True)
        z = z_ref[0, :, sl].astype(F32)
        o_ref[0, :, sl] = (o * _silu(z)).astype(o_ref.dtype)


def _attn_sample(h3, k_cache, v_cache, bias_c, bias_n, layer, d_a):
    bsz, t, _ = h3.shape
    ca = k_cache.shape[3]
    n_heads = d_a // DH_A
    assert k_cache.shape[2:] == (n_heads, ca, DH_A)
    row = lambda col: pl.BlockSpec((1, t, d_a), lambda b: (b, 0, col))
    cache = pl.BlockSpec((None, 1, n_heads, ca, DH_A), lambda b: (layer, b, 0, 0, 0))
    bias = lambda a: pl.BlockSpec((n_heads,) + a.shape[1:], lambda b: (layer, 0, 0))
    return pl.pallas_call(
        functools.partial(_attn_sample_kernel, n_heads=n_heads, dh=DH_A, scale=DH_A ** -0.5),
        out_shape=jax.ShapeDtypeStruct((bsz, t, d_a), BF16),
        grid=(bsz,),
        in_specs=[row(0), row(1), row(2), row(3), cache, cache, bias(bias_c), bias(bias_n)],
        out_specs=pl.BlockSpec((1, t, d_a), lambda b: (b, 0, 0)),
        compiler_params=_params("parallel"),
        name="attn_sample",
    )(h3, h3, h3, h3, k_cache, v_cache, bias_c, bias_n)


def _mem_attn_kernel(q_ref, z_ref, mk_ref, mv_ref, o_ref, *, n_heads, dh, scale, head_axis):
    for hd in range(n_heads):
        sl = slice(hd * dh, (hd + 1) * dh)
        mk = mk_ref[0, :, hd, :] if head_axis else mk_ref[0, :, sl]
        mv = mv_ref[0, :, hd, :] if head_axis else mv_ref[0, :, sl]
        s = lax.dot_general(q_ref[0, :, sl], mk.astype(BF16), NT_DIMS, preferred_element_type=F32)
        o = _softmax_pv([s], [mv.astype(BF16)], scale * LOG2E, False)
        z = z_ref[0, :, sl].astype(F32)
        o_ref[0, :, sl] = (o * _silu(z)).astype(o_ref.dtype)


def _mem_attn(h3, mk, mv, mk_spec, mv_spec, head_axis, d_c, q_col0, tq):
    bsz, t, _ = h3.shape
    tq = min(tq, t)
    assert q_col0 % d_c == 0
    qb = q_col0 // d_c
    dh = d_c // H_C
    return pl.pallas_call(
        functools.partial(_mem_attn_kernel, n_heads=H_C, dh=dh, scale=dh ** -0.5, head_axis=head_axis),
        out_shape=jax.ShapeDtypeStruct((bsz, t, d_c), BF16),
        grid=(bsz, t // tq),
        in_specs=[pl.BlockSpec((1, tq, d_c), lambda b, i: (b, i, qb)),
                  pl.BlockSpec((1, tq, d_c), lambda b, i: (b, i, qb + 1)),
                  mk_spec, mv_spec],
        out_specs=pl.BlockSpec((1, tq, d_c), lambda b, i: (b, i, 0)),
        compiler_params=_params("parallel", "arbitrary"),
        name="mem_attn",
    )(h3, h3, mk, mv)


def _seg_cumsum(x, row, group):
    d = 1
    while d < group:
        x = x + jnp.where((row & (group - 1)) >= d, pltpu.roll(x, d, 0), 0.0)
        d *= 2
    return x


def _hgrn_masks():
    t = np.arange(HG_L)[:, None]
    s = np.arange(HG_L)[None, :]
    m16 = [(t // 16 == i) & (s < 16 * i) for i in range(1, 4)]
    m4 = [(t // 16 == s // 16) & ((t % 16) // 4 == i) & (s % 16 < 4 * i) for i in range(1, 4)]
    base = (t // 4 == s // 4) & (s <= t)
    m1 = np.concatenate(m16, axis=1).astype(np.float32)
    m2 = np.concatenate(m4 + [base], axis=1).astype(np.float32)
    return jnp.asarray(m1, BF16), jnp.asarray(m2, BF16)


def _bcast_rows(ref, offsets, reps):
    return jnp.concatenate(
        [jnp.broadcast_to(ref[o:o + 1, :], (reps, LANE)) for o in offsets], axis=0)


def _hgrn_prep(q, g, k, v, cs_ref):
    L = HG_L
    n = q.shape[0] // L
    starts = [c * L for c in range(n)]
    row = lax.broadcasted_iota(jnp.int32, q.shape, 0)
    c4 = _seg_cumsum(g, row, 4)
    c16 = _seg_cumsum(g, row, 16)
    c16_ref, b_ref, carry_ref = cs_ref.at[0], cs_ref.at[1], cs_ref.at[2]
    c16_ref[...] = c16
    ng = L // 16
    for c, s in enumerate(starts):
        tot = jnp.zeros((1, LANE), F32)
        for j in range(ng):
            carry_ref[c * ng + j:c * ng + j + 1, :] = tot
            if j + 1 < ng:
                tot = tot + c16_ref[s + 16 * j + 15:s + 16 * j + 16, :]
    b = c16 + _bcast_rows(carry_ref, list(range(n * ng)), 16)
    b_ref[...] = b

    k16 = []
    for i in range(1, L // 16):
        ref_rows = _bcast_rows(b_ref, [s + 16 * i - 1 for s in starts], L)
        k16.append((k * jnp.exp2(jnp.minimum(ref_rows - b, 0.0))).astype(BF16))
    k4 = []
    for i in range(1, 4):
        ref_rows = _bcast_rows(c16_ref, [s + 16 * j + 4 * i - 1 for s in starts for j in range(L // 16)], 16)
        k4.append((k * jnp.exp2(jnp.minimum(ref_rows - c16, 0.0))).astype(BF16))
    k4.append((k * jnp.exp2(-c4)).astype(BF16))
    return dict(
        starts=starts, b_ref=b_ref, k16=k16, k4=k4, vb=v.astype(BF16),
        q4=(q * jnp.exp2(c4)).astype(BF16), q16=(q * jnp.exp2(c16)).astype(BF16),
        q64=(q * jnp.exp2(b)).astype(BF16),
        kd=(k * jnp.exp2(_bcast_rows(b_ref, [s + L - 1 for s in starts], L) - b)).astype(BF16))


def _hgrn_scores(p, m1, m2):
    scores = []
    for s in p["starts"]:
        sl = slice(s, s + HG_L)
        a1 = lax.dot_general(p["q16"][sl], jnp.concatenate([x[sl] for x in p["k16"]], axis=0), NT_DIMS,
                             preferred_element_type=F32)
        a2 = lax.dot_general(p["q4"][sl], jnp.concatenate([x[sl] for x in p["k4"]], axis=0), NT_DIMS,
                             preferred_element_type=F32)
        scores.append((a1.astype(BF16) * m1, a2.astype(BF16) * m2))
    return scores


def _hgrn_outputs(p, scores, sts, chain_head):
    L = HG_L
    sts = list(sts)
    outs = []
    for c, s in enumerate(p["starts"]):
        sl = slice(s, s + L)
        a1, a2 = scores[c]
        st = sts[chain_head[c]]
        vc = p["vb"][sl]
        outs.append(
            jnp.dot(a1, jnp.concatenate([vc] * 3, axis=0), preferred_element_type=F32)
            + jnp.dot(a2, jnp.concatenate([vc] * 4, axis=0), preferred_element_type=F32)
            + lax.dot_general(p["q64"][sl], st.astype(BF16), NT_DIMS, preferred_element_type=F32))
        sts[chain_head[c]] = (st * jnp.exp2(p["b_ref"][s + L - 1:s + L, :])
                              + lax.dot_general(vc, p["kd"][sl], TN_DIMS, preferred_element_type=F32))
    return jnp.concatenate(outs, axis=0), sts


def _hgrn_unit(q_ref, f_ref, i_ref, z_ref, lb_ref, gn_ref, m1_ref, m2_ref, y_ref, st_ref, cs_ref,
               *, rows, n_sub, unroll):
    m1 = m1_ref[...]
    m2 = m2_ref[...]
    chains = [(u, p) for u in range(unroll) for p in range(HG_HP)]
    chain_head = [p for _, p in chains]
    lb = jnp.concatenate([jnp.broadcast_to(lb_ref[p], (HG_L, LANE)) for _, p in chains], axis=0)
    gn = jnp.concatenate([jnp.broadcast_to(gn_ref[p], (HG_L, LANE)) for _, p in chains], axis=0)

    def stack(ref, r0s):
        parts = []
        for u, p in chains:
            x = ref[0, r0s[u]:r0s[u] + rows, p * LANE:(p + 1) * LANE].astype(F32)
            if rows < HG_L:
                x = jnp.concatenate([x, jnp.zeros((HG_L - rows, LANE), F32)], axis=0)
            parts.append(x)
        return jnp.concatenate(parts, axis=0)

    def finish(pending, sts):
        p, scores, zr, r0s = pending
        o, sts = _hgrn_outputs(p, scores, sts, chain_head)
        on = o * lax.rsqrt(jnp.mean(o * o, axis=-1, keepdims=True) + EPS) * gn
        y = (on * _silu(zr)).astype(y_ref.dtype)
        for c, (u, hp) in enumerate(chains):
            y_ref[0, r0s[u]:r0s[u] + rows, hp * LANE:(hp + 1) * LANE] = y[c * HG_L:c * HG_L + rows]
        return sts

    sts = [st_ref[p] for p in range(HG_HP)]
    pending = None
    for gi in range(n_sub // unroll):
        r0s = [(gi * unroll + u) * rows for u in range(unroll)]
        qr, fr, vr, zr = (stack(ref, r0s) for ref in (q_ref, f_ref, i_ref, z_ref))
        kf = (1.0 - lb) * _sigmoid(-fr)
        g = jnp.log2(1.0 - kf)
        if rows < HG_L:
            live = (lax.broadcasted_iota(jnp.int32, fr.shape, 0) & (HG_L - 1)) < rows
            kf = jnp.where(live, kf, 0.0)
            g = jnp.where(live, g, 0.0)
        p = _hgrn_prep(_silu(qr), g, kf, vr, cs_ref.at[gi])
        yield
        scores = _hgrn_scores(p, m1, m2)
        if pending is not None:
            sts = finish(pending, sts)
        pending = (p, scores, zr, r0s)
        yield
    sts = finish(pending, sts)
    for p in range(HG_HP):
        st_ref[p] = sts[p]


def _hgrn_kernel(q_ref, f_ref, i_ref, z_ref, lb_ref, gn_ref, s0_ref, m1_ref, m2_ref,
                 y_ref, s_out_ref, st_ref, cs_ref, *, rows, n_sub, unroll):
    tb = pl.program_id(2)

    @pl.when(tb == 0)
    def _():
        for p in range(HG_HP):
            st_ref[p] = s0_ref[0, p].astype(F32).T

    for _ in _hgrn_unit(q_ref, f_ref, i_ref, z_ref, lb_ref, gn_ref, m1_ref, m2_ref, y_ref, st_ref, cs_ref,
                        rows=rows, n_sub=n_sub, unroll=unroll):
        pass

    @pl.when(tb == pl.num_programs(2) - 1)
    def _():
        for p in range(HG_HP):
            s_out_ref[0, p] = st_ref[p].T


def _hgrn(h3, lb, gain, s0, layer, masks, col0, d_b):
    bsz, t, _ = h3.shape
    n_heads = d_b // DK_B
    assert col0 % (HG_HP * LANE) == 0 and n_heads % HG_HP == 0 and DK_B == LANE
    assert s0.shape[1:] == (bsz, n_heads, DK_B, LANE)
    if t % HG_L == 0:
        rows, tb = HG_L, min(HG_TB, t)
    else:
        assert t < HG_L and t % 16 == 0
        rows, tb = t, t
    n_sub = tb // rows
    unroll = HG_UNROLL if n_sub % HG_UNROLL == 0 else 1
    assert t % tb == 0
    w = HG_HP * LANE
    c0 = col0 // w
    col = lambda seg: pl.BlockSpec((1, tb, w), lambda b, hp, i: (b, i, c0 + seg * (n_heads // HG_HP) + hp))
    per_head = pl.BlockSpec((HG_HP, 1, LANE), lambda b, hp, i: (hp, 0, 0))
    m1, m2 = masks
    y, s_new = pl.pallas_call(
        functools.partial(_hgrn_kernel, rows=rows, n_sub=n_sub, unroll=unroll),
        out_shape=(jax.ShapeDtypeStruct((bsz, t, d_b), BF16),
                   jax.ShapeDtypeStruct(s0.shape[1:], F32)),
        grid=(bsz, n_heads // HG_HP, t // tb),
        in_specs=[col(0), col(1), col(2), col(3), per_head, per_head,
                  pl.BlockSpec((None, 1, HG_HP, DK_B, LANE), lambda b, hp, i: (layer, b, hp, 0, 0)),
                  pl.BlockSpec(m1.shape, lambda b, hp, i: (0, 0)),
                  pl.BlockSpec(m2.shape, lambda b, hp, i: (0, 0))],
        out_specs=(pl.BlockSpec((1, tb, w), lambda b, hp, i: (b, i, hp)),
                   pl.BlockSpec((1, HG_HP, DK_B, LANE), lambda b, hp, i: (b, hp, 0, 0))),
        scratch_shapes=[pltpu.VMEM((HG_HP, LANE, DK_B), F32),
                        pltpu.VMEM((n_sub // unroll, 3, unroll * HG_HP * HG_L, LANE), F32)],
        compiler_params=_params("parallel", "parallel", "arbitrary"),
        name="hgrn2",
    )(h3, h3, h3, h3, lb.reshape(n_heads, 1, DK_B), gain.reshape(n_heads, 1, LANE).astype(F32),
      s0, m1, m2)
    return y, s_new


def _layer(x, attend, s0, s0_layer, mem_args, norm_g, w_in, lb, hgrn_g, w_branch, w_out, masks, dims, casts=()):
    d_a, d_b, d_c = dims
    bsz, t, d = x.shape
    m = bsz * t
    x2 = x.reshape(m, d)
    cast_out = ()
    if casts:
        h, *cast_out = _norm_matmul(x2, norm_g, w_in, BF16, 1024, 1024, 64, casts=casts)
        w_branch, w_out = cast_out[:2]
    else:
        h = _matmul(_rmsnorm(x2, norm_g, BF16, 256), w_in, 0, BF16, 1024, 1024)
    h3 = h.reshape(bsz, t, -1)
    ya = attend(h3)
    yb, s_new = _hgrn(h3, lb, hgrn_g, s0, s0_layer, masks, 4 * d_a, d_b)
    yc = _mem_attn(h3, *mem_args, d_c, 4 * d_a + 4 * d_b, 512)
    merged = _merge(ya.reshape(m, d_a), yb.reshape(m, d_b), yc.reshape(m, d_c), h, w_branch, 0,
                    4 * d_a + 4 * d_b + 2 * d_c, 1024, 1024)
    x_new = _matmul(merged, w_out, 0, F32, 1024, 1024, res=x2).reshape(bsz, t, d)
    ka = h3[:, :, d_a:2 * d_a]
    va = h3[:, :, 2 * d_a:3 * d_a]
    return x_new, ka, va, s_new, tuple(cast_out)


def kernel(x_prompt, x_sample, mem_prompt, cache_attn_k, cache_attn_v, state_hgrn, cache_mem_k, cache_mem_v, norm_gain, w_in, rel_bias, lb_logits, hgrn_norm_gain, mem_norm_gain, w_mem_kv, w_branch, w_out, final_norm_gain):
    depth = w_in.shape[0]
    bp, tp, d = x_prompt.shape
    bs, ts, _ = x_sample.shape
    h_a = rel_bias.shape[1]
    d_a = h_a * DH_A
    d_b = lb_logits.shape[1]
    d_c = w_mem_kv.shape[2] // 2
    h_b = d_b // DK_B
    n_mem = mem_prompt.shape[1]
    dims = (d_a, d_b, d_c)
    ca_s = cache_attn_k.shape[2]
    ca_p = min(BAND, tp)

    w_in_l = w_in[:1].astype(BF16)
    w_mem_l = w_mem_kv[:1].astype(BF16)
    cast_cols = 256

    kc_s = jnp.swapaxes(cache_attn_k, 2, 3)
    vc_s = jnp.swapaxes(cache_attn_v, 2, 3)

    lb_all = _lower_bounds(lb_logits)
    masks = _hgrn_masks()
    bias_p = _prompt_bias(rel_bias)
    bias_c, bias_n = _sample_bias(rel_bias, ts, ca_s)
    s0_p = jnp.zeros((1, bp, h_b, DK_B, d_b // h_b), F32)
    mem2 = mem_prompt.reshape(bp * n_mem, d)

    xp, xs = x_prompt, x_sample
    kp_l, vp_l, sp_l, mkp_l, mvp_l, ks_l, vs_l, ss_l = [], [], [], [], [], [], [], []
    for l in range(depth):
        casts = [(w_branch, l, cast_cols), (w_out, l, cast_cols)]
        if l + 1 < depth:
            casts += [(w_in, l + 1, 1024), (w_mem_kv, l + 1, cast_cols // 2)]

        memn = _rmsnorm(mem2, mem_norm_gain[l], BF16, 256)
        mkv = _matmul(memn, w_mem_l, 0, F32, 1024, 1024).reshape(bp, n_mem, 2 * d_c)
        mem_p = (mkv, mkv,
                 pl.BlockSpec((1, n_mem, d_c), lambda b, i: (b, 0, 0)),
                 pl.BlockSpec((1, n_mem, d_c), lambda b, i: (b, 0, 1)), False)
        att_p = lambda h3: _attn_prompt(h3, bias_p, l, d_a)
        xp, ka, va, s_fin, cast_out = _layer(xp, att_p, s0_p, 0, mem_p, norm_gain[l], w_in_l, lb_all[l],
                                             hgrn_norm_gain[l], None, None, masks, dims, casts=casts)
        w_branch_l, w_out_l = cast_out[:2]
        kp_l.append(ka[:, -ca_p:].astype(F32).reshape(bp, ca_p, h_a, DH_A))
        vp_l.append(va[:, -ca_p:].astype(F32).reshape(bp, ca_p, h_a, DH_A))
        sp_l.append(s_fin)
        mkp_l.append(mkv[:, :, :d_c].reshape(bp, n_mem, H_C, d_c // H_C))
        mvp_l.append(mkv[:, :, d_c:].reshape(bp, n_mem, H_C, d_c // H_C))

        mem_blk = pl.BlockSpec((None, 1, n_mem, H_C, d_c // H_C), lambda b, i: (l, b, 0, 0, 0))
        mem_s = (cache_mem_k, cache_mem_v, mem_blk, mem_blk, True)
        att_s = lambda h3: _attn_sample(h3, kc_s, vc_s, bias_c, bias_n, l, d_a)
        xs, ka_s, va_s, s_new, _ = _layer(xs, att_s, state_hgrn, l, mem_s, norm_gain[l], w_in_l, lb_all[l],
                                          hgrn_norm_gain[l], w_branch_l, w_out_l, masks, dims)
        if l + 1 < depth:
            w_in_l, w_mem_l = cast_out[2:4]
        ks_l.append(ka_s.astype(F32).reshape(bs, ts, h_a, DH_A))
        vs_l.append(va_s.astype(F32).reshape(bs, ts, h_a, DH_A))
        ss_l.append(s_new)

    y_prompt = _rmsnorm(xp.reshape(bp * tp, d), final_norm_gain, F32, 256).reshape(bp, tp, d)
    y_sample = _rmsnorm(xs.reshape(bs * ts, d), final_norm_gain, F32, 256).reshape(bs, ts, d)
    return (y_prompt, y_sample,
            jnp.stack(kp_l), jnp.stack(vp_l), jnp.stack(sp_l), jnp.stack(mkp_l), jnp.stack(mvp_l),
            jnp.stack(ks_l), jnp.stack(vs_l), jnp.stack(ss_l))
```

```python
import functools

import numpy as np
import jax
import jax.numpy as jnp
from jax import lax
from jax.experimental import pallas as pl
from jax.experimental.pallas import tpu as pltpu

F32 = jnp.float32
BF16 = jnp.bfloat16

PAST_LEN = 4096
CHUNK = 64
N_LEFT_CHUNKS = 8
BAND = N_LEFT_CHUNKS * CHUNK
DH_A = 128
MAX_REL = 256
DK_B = 128
H_C = 4
EPS = 1e-6
NEG = -1e30
LOG2E = 1.4426950408889634

LANE = 128
VMEM_LIMIT_BYTES = 56 * 1024 * 1024

MM_TM = 1024
MM_TN = 1024
NORM_TM = 256
NORM_RB = 64
CAST_COLS = 256
MEM_TQ = 1024
ATT_TQ = 256
ATT_NKB = BAND // ATT_TQ + 1
HG_L = 64
HG_TB = 2048
HG_HP = 4
HG_UNROLL = 4

NT_DIMS = (((1,), (1,)), ((), ()))
TN_DIMS = (((0,), (0,)), ((), ()))


def _params(*sem):
    return pltpu.CompilerParams(dimension_semantics=sem, vmem_limit_bytes=VMEM_LIMIT_BYTES)


def _sigmoid(x):
    return 0.5 * jnp.tanh(0.5 * x) + 0.5


def _silu(z):
    return z * _sigmoid(z)


def _rmsnorm_kernel(x_ref, g_ref, o_ref):
    x = x_ref[...].astype(F32)
    y = x * lax.rsqrt(jnp.mean(x * x, axis=-1, keepdims=True) + EPS)
    o_ref[...] = (y * g_ref[...]).astype(o_ref.dtype)


def _rmsnorm(x, g, out_dtype, tm):
    m, d = x.shape
    tm = min(tm, m)
    return pl.pallas_call(
        _rmsnorm_kernel,
        out_shape=jax.ShapeDtypeStruct((m, d), out_dtype),
        grid=(m // tm,),
        in_specs=[pl.BlockSpec((tm, d), lambda i: (i, 0)),
                  pl.BlockSpec((1, d), lambda i: (0, 0))],
        out_specs=pl.BlockSpec((tm, d), lambda i: (i, 0)),
        compiler_params=_params("parallel"),
        name="rmsnorm",
    )(x, g.reshape(1, d).astype(F32))


def _mm_kernel(a_ref, b_ref, o_ref):
    o_ref[...] = jnp.dot(a_ref[...], b_ref[...], preferred_element_type=F32).astype(o_ref.dtype)


def _mm_res_kernel(a_ref, b_ref, r_ref, o_ref):
    o_ref[...] = r_ref[...] + jnp.dot(a_ref[...], b_ref[...], preferred_element_type=F32)


def _mm_cast_kernel(a_ref, b_ref, *refs):
    n_cast = (len(refs) - 1) // 2
    src_refs, o_ref, dst_refs = refs[:n_cast], refs[n_cast], refs[n_cast + 1:]
    o_ref[...] = jnp.dot(a_ref[...], b_ref[...], preferred_element_type=F32).astype(o_ref.dtype)
    for src, dst in zip(src_refs, dst_refs):
        dst[...] = src[...].astype(dst.dtype)


def _matmul(a, w, layer, out_dtype, tm, tn, res=None, casts=()):
    m, k = a.shape
    n = w.shape[2]
    tm, tn = min(tm, m), min(tn, n)
    gi, gj = m // tm, n // tn
    in_specs = [pl.BlockSpec((tm, k), lambda i, j: (i, 0)),
                pl.BlockSpec((None, k, tn), lambda i, j: (layer, 0, j))]
    args = [a, w]
    out_shape = [jax.ShapeDtypeStruct((m, n), out_dtype)]
    out_specs = [pl.BlockSpec((tm, tn), lambda i, j: (i, j))]
    kern, name = _mm_kernel, "matmul"
    if res is not None:
        assert not casts
        in_specs.append(pl.BlockSpec((tm, tn), lambda i, j: (i, j)))
        args.append(res)
        kern, name = _mm_res_kernel, "matmul_res"
    if casts:
        kern, name = _mm_cast_kernel, "matmul_cast"
    for src, src_layer, bc in casts:
        _, r, c = src.shape
        br, nj = r // gi, c // bc
        assert br * gi == r and nj * bc == c and nj <= gj
        in_specs.append(pl.BlockSpec((None, br, bc),
                                     lambda i, j, sl=src_layer, nj=nj: (sl, i, jnp.minimum(j, nj - 1))))
        args.append(src)
        out_shape.append(jax.ShapeDtypeStruct((1, r, c), BF16))
        out_specs.append(pl.BlockSpec((None, br, bc), lambda i, j, nj=nj: (0, i, jnp.minimum(j, nj - 1))))
    outs = pl.pallas_call(
        kern,
        out_shape=out_shape,
        grid=(gi, gj),
        in_specs=in_specs,
        out_specs=out_specs,
        compiler_params=_params("parallel", "arbitrary"),
        name=name,
    )(*args)
    return outs if casts else outs[0]


def _norm_mm_cast_kernel(x_ref, g_ref, b_ref, *refs, gi, nxb):
    n_cast = (len(refs) - 2) // 2
    src_refs, o_ref, dst_refs, xn_ref = refs[:n_cast], refs[n_cast], refs[n_cast + 1:-1], refs[-1]
    i, j = pl.program_id(0), pl.program_id(1)
    rb = x_ref.shape[0]

    def norm_block():
        x = x_ref[...].astype(F32)
        y = x * lax.rsqrt(jnp.mean(x * x, axis=-1, keepdims=True) + EPS)
        r0 = pl.multiple_of(jnp.minimum(j, nxb - 1) * rb, rb)
        xn_ref[i % 2, pl.ds(r0, rb), :] = (y * g_ref[...]).astype(xn_ref.dtype)

    def matmul_tile():
        o_ref[...] = jnp.dot(xn_ref[(i + 1) % 2], b_ref[...], preferred_element_type=F32).astype(o_ref.dtype)
        for src, dst in zip(src_refs, dst_refs):
            dst[...] = src[...].astype(dst.dtype)

    @pl.when(i == 0)
    def _():
        norm_block()

    @pl.when(jnp.logical_and(i > 0, i < gi))
    def _():
        matmul_tile()
        norm_block()

    @pl.when(i == gi)
    def _():
        matmul_tile()


def _norm_matmul(x, g, w, out_dtype, tm, tn, rb, casts=()):
    m, k = x.shape
    n = w.shape[2]
    gi, gj = m // tm, n // tn
    nxb = tm // rb
    assert gi * tm == m and gj * tn == n and nxb * rb == tm and nxb <= gj
    live = lambda i, v: jnp.where(i > 0, v, 0)
    in_specs = [pl.BlockSpec((rb, k), lambda i, j: (jnp.minimum(i, gi - 1) * nxb + jnp.minimum(j, nxb - 1), 0)),
                pl.BlockSpec((1, k), lambda i, j: (0, 0)),
                pl.BlockSpec((None, k, tn), lambda i, j: (0, 0, live(i, j)))]
    args = [x, g.reshape(1, k).astype(F32), w]
    out_shape = [jax.ShapeDtypeStruct((m, n), out_dtype)]
    out_specs = [pl.BlockSpec((tm, tn), lambda i, j: (jnp.maximum(i - 1, 0), live(i, j)))]
    for src, src_layer, bc in casts:
        _, r, c = src.shape
        br, nj = r // gi, c // bc
        assert br * gi == r and nj * bc == c and nj <= gj
        in_specs.append(pl.BlockSpec(
            (None, br, bc),
            lambda i, j, sl=src_layer, nj=nj: (sl, jnp.maximum(i - 1, 0), live(i, jnp.minimum(j, nj - 1)))))
        args.append(src)
        out_shape.append(jax.ShapeDtypeStruct((1, r, c), BF16))
        out_specs.append(pl.BlockSpec(
            (None, br, bc), lambda i, j, nj=nj: (0, jnp.maximum(i - 1, 0), live(i, jnp.minimum(j, nj - 1)))))
    return pl.pallas_call(
        functools.partial(_norm_mm_cast_kernel, gi=gi, nxb=nxb),
        out_shape=out_shape,
        grid=(gi + 1, gj),
        in_specs=in_specs,
        out_specs=out_specs,
        scratch_shapes=[pltpu.VMEM((2, tm, k), BF16)],
        compiler_params=_params("arbitrary", "arbitrary"),
        name="norm_matmul_cast",
    )(*args)


def _merge_kernel(ya_ref, yb_ref, yc_ref, wa_ref, wb_ref, wc_ref, ga_ref, gb_ref, gc_ref, o_ref):
    def part(y_ref, w_ref, g_ref):
        gate = _sigmoid(g_ref[...].astype(F32))
        return gate * jnp.dot(y_ref[...], w_ref[...], preferred_element_type=F32)

    o_ref[...] = (part(ya_ref, wa_ref, ga_ref) + part(yb_ref, wb_ref, gb_ref)
                  + part(yc_ref, wc_ref, gc_ref)).astype(o_ref.dtype)


def _merge(ya, yb, yc, h, w_branch, layer, gate_col0, tm, tn):
    m, d_a = ya.shape
    d_b, d_c = yb.shape[1], yc.shape[1]
    d = w_branch.shape[2]
    tm = min(tm, m)
    assert d_a == d_b and (d_a + d_b) % d_c == 0 and gate_col0 % tn == 0 and d % tn == 0
    g0, gstep = gate_col0 // tn, d // tn
    return pl.pallas_call(
        _merge_kernel,
        out_shape=jax.ShapeDtypeStruct((m, d), BF16),
        grid=(m // tm, d // tn),
        in_specs=[pl.BlockSpec((tm, d_a), lambda i, j: (i, 0)),
                  pl.BlockSpec((tm, d_b), lambda i, j: (i, 0)),
                  pl.BlockSpec((tm, d_c), lambda i, j: (i, 0)),
                  pl.BlockSpec((None, d_a, tn), lambda i, j: (layer, 0, j)),
                  pl.BlockSpec((None, d_b, tn), lambda i, j: (layer, 1, j)),
                  pl.BlockSpec((None, d_c, tn), lambda i, j: (layer, (d_a + d_b) // d_c, j)),
                  pl.BlockSpec((tm, tn), lambda i, j: (i, g0 + j)),
                  pl.BlockSpec((tm, tn), lambda i, j: (i, g0 + gstep + j)),
                  pl.BlockSpec((tm, tn), lambda i, j: (i, g0 + 2 * gstep + j))],
        out_specs=pl.BlockSpec((tm, tn), lambda i, j: (i, j)),
        compiler_params=_params("parallel", "arbitrary"),
        name="merge",
    )(ya, yb, yc, w_branch, w_branch, w_branch, h, h, h)


def _lower_bound_kernel(x_ref, o_ref):
    x = x_ref[...].astype(F32)
    e = jnp.exp(x - jnp.max(x, axis=0, keepdims=True))
    sm = e / jnp.sum(e, axis=0, keepdims=True)
    row = lax.broadcasted_iota(jnp.int32, x.shape, 0)
    acc = jnp.zeros_like(x)
    for i in range(1, x.shape[0]):
        acc = acc + jnp.where(row >= i, sm[i:i + 1, :], 0.0)
    o_ref[...] = acc


def _lower_bounds(lb_logits):
    return pl.pallas_call(
        _lower_bound_kernel,
        out_shape=jax.ShapeDtypeStruct(lb_logits.shape, F32),
        name="hgrn_lower_bound",
    )(lb_logits)


def _toeplitz_kernel(w_ref, mask_ref, o_ref, *, mult):
    rows, cols = o_ref.shape[-2:]
    p = w_ref.shape[-1]
    for j in range(o_ref.shape[1]):
        x = jnp.broadcast_to(w_ref[0, j], (rows, p))
        x = pltpu.roll(x, 0, 1, stride=1, stride_axis=0)
        o_ref[0, j] = x[:, :cols] * mult + mask_ref[j]


def _rel_bias_blocks(rel_bias, rows, cols, deltas, masks):
    p = -(-(rows + cols - 1) // LANE) * LANE
    k = np.arange(p)
    diag = np.where(k < cols, -k, p - k)
    idx = np.stack([np.clip(diag + dl, -MAX_REL, MAX_REL) + MAX_REL for dl in deltas])
    n_tab = rel_bias.shape[0] * rel_bias.shape[1]
    w = jnp.take(rel_bias.astype(F32).reshape(n_tab, -1), jnp.asarray(idx.reshape(-1)), axis=1)
    w = w.reshape(n_tab, len(deltas), 1, p)
    return pl.pallas_call(
        functools.partial(_toeplitz_kernel, mult=DH_A ** 0.5),
        out_shape=jax.ShapeDtypeStruct((n_tab, len(deltas), rows, cols), F32),
        grid=(n_tab,),
        in_specs=[pl.BlockSpec((1, len(deltas), 1, p), lambda i: (i, 0, 0, 0)),
                  pl.BlockSpec((len(deltas), rows, cols), lambda i: (0, 0, 0))],
        out_specs=pl.BlockSpec((1, len(deltas), rows, cols), lambda i: (i, 0, 0, 0)),
        compiler_params=_params("parallel"),
        name="rel_bias_blocks",
    )(w, jnp.asarray(masks, F32))


def _band_mask(q_pos, k_pos):
    qc, kc = q_pos // CHUNK, k_pos // CHUNK
    valid = (kc[None, :] <= qc[:, None]) & (kc[None, :] >= qc[:, None] - N_LEFT_CHUNKS)
    return np.where(valid, 0.0, NEG)


def _prompt_bias(rel_bias):
    r = np.arange(ATT_TQ)
    deltas = [(ATT_NKB - 1 - j) * ATT_TQ for j in range(ATT_NKB)]
    base = BAND
    masks = np.stack([_band_mask(base + r, base - dl + r) for dl in deltas])
    return _rel_bias_blocks(rel_bias, ATT_TQ, ATT_TQ, deltas, masks)


def _sample_bias(rel_bias, t, ca):
    q_pos = PAST_LEN + np.arange(t)
    k_pos = np.concatenate([PAST_LEN - ca + np.arange(ca), q_pos])
    mask = _band_mask(q_pos, k_pos) + np.where(k_pos >= 0, 0.0, NEG)[None, :]
    b = _rel_bias_blocks(rel_bias, t, ca + t, [ca], mask[None])[:, 0]
    return b[:, :, :ca], b[:, :, ca:]


def _softmax_pv(scores, values, c, ones_col):
    if all(s.shape == scores[0].shape for s in scores):
        mx = scores[0]
        for s in scores[1:]:
            mx = jnp.maximum(mx, s)
        m = jnp.max(mx, axis=-1, keepdims=True)
    else:
        m = jnp.max(scores[0], axis=-1, keepdims=True)
        for s in scores[1:]:
            m = jnp.maximum(m, jnp.max(s, axis=-1, keepdims=True))
    dh = values[0].shape[1]
    l = None
    o = None
    for s, v in zip(scores, values):
        p = jnp.exp2((s - m) * c).astype(BF16)
        if ones_col:
            v = jnp.concatenate([v, jnp.ones_like(v)], axis=1)
        else:
            ps = jnp.sum(p.astype(F32), axis=-1, keepdims=True)
            l = ps if l is None else l + ps
        pv = jnp.dot(p, v, preferred_element_type=F32)
        o = pv if o is None else o + pv
    if ones_col:
        return o[:, :dh] / o[:, dh:]
    return o / l


def _attn_prompt_kernel(q_ref, *refs, n_heads, dh, nkb, scale):
    k_refs = refs[:nkb]
    v_refs = refs[nkb:2 * nkb]
    z_ref, bias_ref, o_ref = refs[2 * nkb:]
    t = pl.program_id(1)

    def head_scores(hd):
        sl = slice(hd * dh, (hd + 1) * dh)
        q = q_ref[0, :, sl]
        scores = []
        for j in range(nkb):
            s = lax.dot_general(q, k_refs[j][0, :, sl], NT_DIMS, preferred_element_type=F32)
            s = s + bias_ref[hd, j]
            if j < nkb - 1:
                s = jnp.where(t >= nkb - 1 - j, s, NEG)
            scores.append(s)
        return scores

    scores = head_scores(0)
    for hd in range(n_heads):
        nxt = head_scores(hd + 1) if hd + 1 < n_heads else None
        sl = slice(hd * dh, (hd + 1) * dh)
        o = _softmax_pv(scores, [v_refs[j][0, :, sl] for j in range(nkb)], scale * LOG2E, True)
        z = z_ref[0, :, sl].astype(F32)
        o_ref[0, :, sl] = (o * _silu(z)).astype(o_ref.dtype)
        scores = nxt


def _attn_prompt(h3, bias, layer, d_a):
    bsz, t, _ = h3.shape
    n_heads = d_a // DH_A
    nkb = ATT_NKB
    assert t % ATT_TQ == 0 and ATT_TQ % CHUNK == 0 and BAND % ATT_TQ == 0

    def kv_spec(col, j):
        return pl.BlockSpec((1, ATT_TQ, d_a),
                            lambda b, i: (b, jnp.maximum(i - (nkb - 1 - j), 0), col))

    in_specs = ([pl.BlockSpec((1, ATT_TQ, d_a), lambda b, i: (b, i, 0))]
                + [kv_spec(1, j) for j in range(nkb)]
                + [kv_spec(2, j) for j in range(nkb)]
                + [pl.BlockSpec((1, ATT_TQ, d_a), lambda b, i: (b, i, 3)),
                   pl.BlockSpec((n_heads,) + bias.shape[1:], lambda b, i: (layer, 0, 0, 0))])
    return pl.pallas_call(
        functools.partial(_attn_prompt_kernel, n_heads=n_heads, dh=DH_A, nkb=nkb, scale=DH_A ** -0.5),
        out_shape=jax.ShapeDtypeStruct((bsz, t, d_a), BF16),
        grid=(bsz, t // ATT_TQ),
        in_specs=in_specs,
        out_specs=pl.BlockSpec((1, ATT_TQ, d_a), lambda b, i: (b, i, 0)),
        compiler_params=_params("parallel", "arbitrary"),
        name="attn_prompt",
    )(h3, *([h3] * (2 * nkb)), h3, bias)


def _attn_sample_kernel(q_ref, k_ref, v_ref, z_ref, kc_ref, vc_ref, bc_ref, bn_ref, o_ref, *, n_heads, dh, scale):
    for hd in range(n_heads):
        sl = slice(hd * dh, (hd + 1) * dh)
        q = q_ref[0, :, sl]
        s_c = lax.dot_general(q, kc_ref[0, hd].astype(BF16), NT_DIMS, preferred_element_type=F32)
        s_n = lax.dot_general(q, k_ref[0, :, sl], NT_DIMS, preferred_element_type=F32)
        scores = [s_c + bc_ref[hd], s_n + bn_ref[hd]]
        o = _softmax_pv(scores, [vc_ref[0, hd].astype(BF16), v_ref[0, :, sl]], scale * LOG2E, True)
        z = z_ref[0, :, sl].astype(F32)
        o_ref[0, :, sl] = (o * _silu(z)).astype(o_ref.dtype)


def _attn_sample(h3, k_cache, v_cache, bias_c, bias_n, layer, d_a):
    bsz, t, _ = h3.shape
    ca = k_cache.shape[3]
    n_heads = d_a // DH_A
    assert k_cache.shape[2:] == (n_heads, ca, DH_A)
    row = lambda col: pl.BlockSpec((1, t, d_a), lambda b: (b, 0, col))
    cache = pl.BlockSpec((None, 1, n_heads, ca, DH_A), lambda b: (layer, b, 0, 0, 0))
    bias = lambda a: pl.BlockSpec((n_heads,) + a.shape[1:], lambda b: (layer, 0, 0))
    return pl.pallas_call(
        functools.partial(_attn_sample_kernel, n_heads=n_heads, dh=DH_A, scale=DH_A ** -0.5),
        out_shape=jax.ShapeDtypeStruct((bsz, t, d_a), BF16),
        grid=(bsz,),
        in_specs=[row(0), row(1), row(2), row(3), cache, cache, bias(bias_c), bias(bias_n)],
        out_specs=pl.BlockSpec((1, t, d_a), lambda b: (b, 0, 0)),
        compiler_params=_params("parallel"),
        name="attn_sample",
    )(h3, h3, h3, h3, k_cache, v_cache, bias_c, bias_n)


def _mem_attn_kernel(q_ref, z_ref, mk_ref, mv_ref, o_ref, *, n_heads, dh, scale, head_axis):
    for hd in range(n_heads):
        sl = slice(hd * dh, (hd + 1) * dh)
        mk = mk_ref[0, :, hd, :] if head_axis else mk_ref[0, :, sl]
        mv = mv_ref[0, :, hd, :] if head_axis else mv_ref[0, :, sl]
        s = lax.dot_general(q_ref[0, :, sl], mk.astype(BF16), NT_DIMS, preferred_element_type=F32)
        o = _softmax_pv([s], [mv.astype(BF16)], scale * LOG2E, False)
        z = z_ref[0, :, sl].astype(F32)
        o_ref[0, :, sl] = (o * _silu(z)).astype(o_ref.dtype)


def _mem_attn(h3, mk, mv, mk_spec, mv_spec, head_axis, d_c, q_col0, tq):
    bsz, t, _ = h3.shape
    tq = min(tq, t)
    assert q_col0 % d_c == 0
    qb = q_col0 // d_c
    dh = d_c // H_C
    return pl.pallas_call(
        functools.partial(_mem_attn_kernel, n_heads=H_C, dh=dh, scale=dh ** -0.5, head_axis=head_axis),
        out_shape=jax.ShapeDtypeStruct((bsz, t, d_c), BF16),
        grid=(bsz, t // tq),
        in_specs=[pl.BlockSpec((1, tq, d_c), lambda b, i: (b, i, qb)),
                  pl.BlockSpec((1, tq, d_c), lambda b, i: (b, i, qb + 1)),
                  mk_spec, mv_spec],
        out_specs=pl.BlockSpec((1, tq, d_c), lambda b, i: (b, i, 0)),
        compiler_params=_params("parallel", "arbitrary"),
        name="mem_attn",
    )(h3, h3, mk, mv)


def _seg_cumsum(x, row, group):
    d = 1
    while d < group:
        x = x + jnp.where((row & (group - 1)) >= d, pltpu.roll(x, d, 0), 0.0)
        d *= 2
    return x


def _hgrn_masks():
    t = np.arange(HG_L)[:, None]
    s = np.arange(HG_L)[None, :]
    m16 = [(t // 16 == i) & (s < 16 * i) for i in range(1, 4)]
    m4 = [(t // 16 == s // 16) & ((t % 16) // 4 == i) & (s % 16 < 4 * i) for i in range(1, 4)]
    base = (t // 4 == s // 4) & (s <= t)
    m1 = np.concatenate(m16, axis=1).astype(np.float32)
    m2 = np.concatenate(m4 + [base], axis=1).astype(np.float32)
    return jnp.asarray(m1, BF16), jnp.asarray(m2, BF16)


def _bcast_rows(ref, offsets, reps):
    return jnp.concatenate(
        [jnp.broadcast_to(ref[o:o + 1, :], (reps, LANE)) for o in offsets], axis=0)


def _hgrn_prep(q, g, k, v, cs_ref):
    L = HG_L
    n = q.shape[0] // L
    starts = [c * L for c in range(n)]
    row = lax.broadcasted_iota(jnp.int32, q.shape, 0)
    c4 = _seg_cumsum(g, row, 4)
    c16 = _seg_cumsum(g, row, 16)
    c16_ref, b_ref, carry_ref = cs_ref.at[0], cs_ref.at[1], cs_ref.at[2]
    c16_ref[...] = c16
    ng = L // 16
    for c, s in enumerate(starts):
        tot = jnp.zeros((1, LANE), F32)
        for j in range(ng):
            carry_ref[c * ng + j:c * ng + j + 1, :] = tot
            if j + 1 < ng:
                tot = tot + c16_ref[s + 16 * j + 15:s + 16 * j + 16, :]
    b = c16 + _bcast_rows(carry_ref, list(range(n * ng)), 16)
    b_ref[...] = b

    k16 = []
    for i in range(1, L // 16):
        ref_rows = _bcast_rows(b_ref, [s + 16 * i - 1 for s in starts], L)
        k16.append((k * jnp.exp2(jnp.minimum(ref_rows - b, 0.0))).astype(BF16))
    k4 = []
    for i in range(1, 4):
        ref_rows = _bcast_rows(c16_ref, [s + 16 * j + 4 * i - 1 for s in starts for j in range(L // 16)], 16)
        k4.append((k * jnp.exp2(jnp.minimum(ref_rows - c16, 0.0))).astype(BF16))
    k4.append((k * jnp.exp2(-c4)).astype(BF16))
    return dict(
        starts=starts, b_ref=b_ref, k16=k16, k4=k4, vb=v.astype(BF16),
        q4=(q * jnp.exp2(c4)).astype(BF16), q16=(q * jnp.exp2(c16)).astype(BF16),
        q64=(q * jnp.exp2(b)).astype(BF16),
        kd=(k * jnp.exp2(_bcast_rows(b_ref, [s + L - 1 for s in starts], L) - b)).astype(BF16))


def _hgrn_scores(p, m1, m2):
    scores = []
    for s in p["starts"]:
        sl = slice(s, s + HG_L)
        a1 = lax.dot_general(p["q16"][sl], jnp.concatenate([x[sl] for x in p["k16"]], axis=0), NT_DIMS,
                             preferred_element_type=F32)
        a2 = lax.dot_general(p["q4"][sl], jnp.concatenate([x[sl] for x in p["k4"]], axis=0), NT_DIMS,
                             preferred_element_type=F32)
        scores.append((a1.astype(BF16) * m1, a2.astype(BF16) * m2))
    return scores


def _hgrn_outputs(p, scores, sts, chain_head):
    L = HG_L
    sts = list(sts)
    outs = []
    for c, s in enumerate(p["starts"]):
        sl = slice(s, s + L)
        a1, a2 = scores[c]
        st = sts[chain_head[c]]
        vc = p["vb"][sl]
        outs.append(
            jnp.dot(a1, jnp.concatenate([vc] * 3, axis=0), preferred_element_type=F32)
            + jnp.dot(a2, jnp.concatenate([vc] * 4, axis=0), preferred_element_type=F32)
            + lax.dot_general(p["q64"][sl], st.astype(BF16), NT_DIMS, preferred_element_type=F32))
        sts[chain_head[c]] = (st * jnp.exp2(p["b_ref"][s + L - 1:s + L, :])
                              + lax.dot_general(vc, p["kd"][sl], TN_DIMS, preferred_element_type=F32))
    return jnp.concatenate(outs, axis=0), sts


def _hgrn_unit(q_ref, f_ref, i_ref, z_ref, lb_ref, gn_ref, m1_ref, m2_ref, y_ref, st_ref, cs_ref,
               *, rows, n_sub, unroll):
    m1 = m1_ref[...]
    m2 = m2_ref[...]
    chains = [(u, p) for u in range(unroll) for p in range(HG_HP)]
    chain_head = [p for _, p in chains]
    lb = jnp.concatenate([jnp.broadcast_to(lb_ref[p], (HG_L, LANE)) for _, p in chains], axis=0)
    gn = jnp.concatenate([jnp.broadcast_to(gn_ref[p], (HG_L, LANE)) for _, p in chains], axis=0)

    def stack(ref, r0s):
        parts = []
        for u, p in chains:
            x = ref[0, r0s[u]:r0s[u] + rows, p * LANE:(p + 1) * LANE].astype(F32)
            if rows < HG_L:
                x = jnp.concatenate([x, jnp.zeros((HG_L - rows, LANE), F32)], axis=0)
            parts.append(x)
        return jnp.concatenate(parts, axis=0)

    def finish(pending, sts):
        p, scores, zr, r0s = pending
        o, sts = _hgrn_outputs(p, scores, sts, chain_head)
        on = o * lax.rsqrt(jnp.mean(o * o, axis=-1, keepdims=True) + EPS) * gn
        y = (on * _silu(zr)).astype(y_ref.dtype)
        for c, (u, hp) in enumerate(chains):
            y_ref[0, r0s[u]:r0s[u] + rows, hp * LANE:(hp + 1) * LANE] = y[c * HG_L:c * HG_L + rows]
        return sts

    sts = [st_ref[p] for p in range(HG_HP)]
    pending = None
    for gi in range(n_sub // unroll):
        r0s = [(gi * unroll + u) * rows for u in range(unroll)]
        qr, fr, vr, zr = (stack(ref, r0s) for ref in (q_ref, f_ref, i_ref, z_ref))
        kf = (1.0 - lb) * _sigmoid(-fr)
        g = jnp.log2(1.0 - kf)
        if rows < HG_L:
            live = (lax.broadcasted_iota(jnp.int32, fr.shape, 0) & (HG_L - 1)) < rows
            kf = jnp.where(live, kf, 0.0)
            g = jnp.where(live, g, 0.0)
        p = _hgrn_prep(_silu(qr), g, kf, vr, cs_ref.at[gi])
        yield
        scores = _hgrn_scores(p, m1, m2)
        if pending is not None:
            sts = finish(pending, sts)
        pending = (p, scores, zr, r0s)
        yield
    sts = finish(pending, sts)
    for p in range(HG_HP):
        st_ref[p] = sts[p]


def _hgrn_kernel(q_ref, f_ref, i_ref, z_ref, lb_ref, gn_ref, s0_ref, m1_ref, m2_ref,
                 y_ref, s_out_ref, st_ref, cs_ref, *, rows, n_sub, unroll):
    tb = pl.program_id(2)

    @pl.when(tb == 0)
    def _():
        for p in range(HG_HP):
            st_ref[p] = s0_ref[0, p].astype(F32).T

    for _ in _hgrn_unit(q_ref, f_ref, i_ref, z_ref, lb_ref, gn_ref, m1_ref, m2_ref, y_ref, st_ref, cs_ref,
                        rows=rows, n_sub=n_sub, unroll=unroll):
        pass

    @pl.when(tb == pl.num_programs(2) - 1)
    def _():
        for p in range(HG_HP):
            s_out_ref[0, p] = st_ref[p].T


def _hgrn(h3, lb, gain, s0, layer, masks, col0, d_b):
    bsz, t, _ = h3.shape
    n_heads = d_b // DK_B
    assert col0 % (HG_HP * LANE) == 0 and n_heads % HG_HP == 0 and DK_B == LANE
    assert s0.shape[1:] == (bsz, n_heads, DK_B, LANE)
    if t % HG_L == 0:
        rows, tb = HG_L, min(HG_TB, t)
    else:
        assert t < HG_L and t % 16 == 0
        rows, tb = t, t
    n_sub = tb // rows
    unroll = HG_UNROLL if n_sub % HG_UNROLL == 0 else 1
    assert t % tb == 0
    w = HG_HP * LANE
    c0 = col0 // w
    col = lambda seg: pl.BlockSpec((1, tb, w), lambda b, hp, i: (b, i, c0 + seg * (n_heads // HG_HP) + hp))
    per_head = pl.BlockSpec((HG_HP, 1, LANE), lambda b, hp, i: (hp, 0, 0))
    m1, m2 = masks
    y, s_new = pl.pallas_call(
        functools.partial(_hgrn_kernel, rows=rows, n_sub=n_sub, unroll=unroll),
        out_shape=(jax.ShapeDtypeStruct((bsz, t, d_b), BF16),
                   jax.ShapeDtypeStruct(s0.shape[1:], F32)),
        grid=(bsz, n_heads // HG_HP, t // tb),
        in_specs=[col(0), col(1), col(2), col(3), per_head, per_head,
                  pl.BlockSpec((None, 1, HG_HP, DK_B, LANE), lambda b, hp, i: (layer, b, hp, 0, 0)),
                  pl.BlockSpec(m1.shape, lambda b, hp, i: (0, 0)),
                  pl.BlockSpec(m2.shape, lambda b, hp, i: (0, 0))],
        out_specs=(pl.BlockSpec((1, tb, w), lambda b, hp, i: (b, i, hp)),
                   pl.BlockSpec((1, HG_HP, DK_B, LANE), lambda b, hp, i: (b, hp, 0, 0))),
        scratch_shapes=[pltpu.VMEM((HG_HP, LANE, DK_B), F32),
                        pltpu.VMEM((n_sub // unroll, 3, unroll * HG_HP * HG_L, LANE), F32)],
        compiler_params=_params("parallel", "parallel", "arbitrary"),
        name="hgrn2",
    )(h3, h3, h3, h3, lb.reshape(n_heads, 1, DK_B), gain.reshape(n_heads, 1, LANE).astype(F32),
      s0, m1, m2)
    return y, s_new


def _layer(x, attend, s0, s0_layer, mem_args, norm_g, w_in, lb, hgrn_g, w_branch, w_out, masks, dims, casts=()):
    d_a, d_b, d_c = dims
    bsz, t, d = x.shape
    m = bsz * t
    x2 = x.reshape(m, d)
    cast_out = ()
    if casts:
        h, *cast_out = _norm_matmul(x2, norm_g, w_in, BF16, MM_TM, MM_TN, NORM_RB, casts=casts)
        w_branch, w_out = cast_out[:2]
    else:
        h = _matmul(_rmsnorm(x2, norm_g, BF16, NORM_TM), w_in, 0, BF16, MM_TM, MM_TN)
    h3 = h.reshape(bsz, t, -1)
    ya = attend(h3)
    yb, s_new = _hgrn(h3, lb, hgrn_g, s0, s0_layer, masks, 4 * d_a, d_b)
    yc = _mem_attn(h3, *mem_args, d_c, 4 * d_a + 4 * d_b, MEM_TQ)
    merged = _merge(ya.reshape(m, d_a), yb.reshape(m, d_b), yc.reshape(m, d_c), h, w_branch, 0,
                    4 * d_a + 4 * d_b + 2 * d_c, MM_TM, MM_TN)
    x_new = _matmul(merged, w_out, 0, F32, MM_TM, MM_TN, res=x2).reshape(bsz, t, d)
    ka = h3[:, :, d_a:2 * d_a]
    va = h3[:, :, 2 * d_a:3 * d_a]
    return x_new, ka, va, s_new, tuple(cast_out)


def kernel(x_prompt, x_sample, mem_prompt, cache_attn_k, cache_attn_v, state_hgrn, cache_mem_k, cache_mem_v, norm_gain, w_in, rel_bias, lb_logits, hgrn_norm_gain, mem_norm_gain, w_mem_kv, w_branch, w_out, final_norm_gain):
    depth = w_in.shape[0]
    bp, tp, d = x_prompt.shape
    bs, ts, _ = x_sample.shape
    h_a = rel_bias.shape[1]
    d_a = h_a * DH_A
    d_b = lb_logits.shape[1]
    d_c = w_mem_kv.shape[2] // 2
    h_b = d_b // DK_B
    n_mem = mem_prompt.shape[1]
    dims = (d_a, d_b, d_c)
    ca_s = cache_attn_k.shape[2]
    ca_p = min(BAND, tp)

    w_in_l = w_in[:1].astype(BF16)
    w_mem_l = w_mem_kv[:1].astype(BF16)

    kc_s = jnp.swapaxes(cache_attn_k, 2, 3)
    vc_s = jnp.swapaxes(cache_attn_v, 2, 3)

    lb_all = _lower_bounds(lb_logits)
    masks = _hgrn_masks()
    bias_p = _prompt_bias(rel_bias)
    bias_c, bias_n = _sample_bias(rel_bias, ts, ca_s)
    s0_p = jnp.zeros((1, bp, h_b, DK_B, d_b // h_b), F32)
    mem2 = mem_prompt.reshape(bp * n_mem, d)

    xp, xs = x_prompt, x_sample
    kp_l, vp_l, sp_l, mkp_l, mvp_l, ks_l, vs_l, ss_l = [], [], [], [], [], [], [], []
    for l in range(depth):
        casts = [(w_branch, l, CAST_COLS), (w_out, l, CAST_COLS)]
        if l + 1 < depth:
            casts += [(w_in, l + 1, MM_TN), (w_mem_kv, l + 1, CAST_COLS // 2)]

        memn = _rmsnorm(mem2, mem_norm_gain[l], BF16, NORM_TM)
        mkv = _matmul(memn, w_mem_l, 0, F32, MM_TM, MM_TN).reshape(bp, n_mem, 2 * d_c)
        mem_p = (mkv, mkv,
                 pl.BlockSpec((1, n_mem, d_c), lambda b, i: (b, 0, 0)),
                 pl.BlockSpec((1, n_mem, d_c), lambda b, i: (b, 0, 1)), False)
        att_p = lambda h3: _attn_prompt(h3, bias_p, l, d_a)
        xp, ka, va, s_fin, cast_out = _layer(xp, att_p, s0_p, 0, mem_p, norm_gain[l], w_in_l, lb_all[l],
                                             hgrn_norm_gain[l], None, None, masks, dims, casts=casts)
        w_branch_l, w_out_l = cast_out[:2]
        kp_l.append(ka[:, -ca_p:].astype(F32).reshape(bp, ca_p, h_a, DH_A))
        vp_l.append(va[:, -ca_p:].astype(F32).reshape(bp, ca_p, h_a, DH_A))
        sp_l.append(s_fin)
        mkp_l.append(mkv[:, :, :d_c].reshape(bp, n_mem, H_C, d_c // H_C))
        mvp_l.append(mkv[:, :, d_c:].reshape(bp, n_mem, H_C, d_c // H_C))

        mem_blk = pl.BlockSpec((None, 1, n_mem, H_C, d_c // H_C), lambda b, i: (l, b, 0, 0, 0))
        mem_s = (cache_mem_k, cache_mem_v, mem_blk, mem_blk, True)
        att_s = lambda h3: _attn_sample(h3, kc_s, vc_s, bias_c, bias_n, l, d_a)
        xs, ka_s, va_s, s_new, _ = _layer(xs, att_s, state_hgrn, l, mem_s, norm_gain[l], w_in_l, lb_all[l],
                                          hgrn_norm_gain[l], w_branch_l, w_out_l, masks, dims)
        if l + 1 < depth:
            w_in_l, w_mem_l = cast_out[2:4]
        ks_l.append(ka_s.astype(F32).reshape(bs, ts, h_a, DH_A))
        vs_l.append(va_s.astype(F32).reshape(bs, ts, h_a, DH_A))
        ss_l.append(s_new)

    y_prompt = _rmsnorm(xp.reshape(bp * tp, d), final_norm_gain, F32, NORM_TM).reshape(bp, tp, d)
    y_sample = _rmsnorm(xs.reshape(bs * ts, d), final_norm_gain, F32, NORM_TM).reshape(bs, ts, d)
    return (y_prompt, y_sample,
            jnp.stack(kp_l), jnp.stack(vp_l), jnp.stack(sp_l), jnp.stack(mkp_l), jnp.stack(mvp_l),
            jnp.stack(ks_l), jnp.stack(vs_l), jnp.stack(ss_l))
```

```python
import functools

import numpy as np
import jax
import jax.numpy as jnp
from jax import lax
from jax.experimental import pallas as pl
from jax.experimental.pallas import tpu as pltpu

F32 = jnp.float32
BF16 = jnp.bfloat16

PAST_LEN = 4096
CHUNK = 64
N_LEFT_CHUNKS = 8
BAND = N_LEFT_CHUNKS * CHUNK
DH_A = 128
MAX_REL = 256
DK_B = 128
H_C = 4
EPS = 1e-6
NEG = -1e30
LOG2E = 1.4426950408889634

LANE = 128
VMEM_LIMIT_BYTES = 56 * 1024 * 1024

MM_TM = 1024
MM_TN = 1024
NORM_TM = 256
NORM_RB = 64
CAST_COLS = 256
MEM_TQ = 1024
ATT_TQ = 256
ATT_NKB = BAND // ATT_TQ + 1
HG_L = 64
HG_TB = 2048
HG_HP = 4
HG_UNROLL = 4

NT_DIMS = (((1,), (1,)), ((), ()))
TN_DIMS = (((0,), (0,)), ((), ()))


def _params(*sem):
    return pltpu.CompilerParams(dimension_semantics=sem, vmem_limit_bytes=VMEM_LIMIT_BYTES)


def _sigmoid(x):
    return 0.5 * jnp.tanh(0.5 * x) + 0.5


def _silu(z):
    return z * _sigmoid(z)


def _rmsnorm_kernel(x_ref, g_ref, o_ref):
    x = x_ref[...].astype(F32)
    y = x * lax.rsqrt(jnp.mean(x * x, axis=-1, keepdims=True) + EPS)
    o_ref[...] = (y * g_ref[...]).astype(o_ref.dtype)


def _rmsnorm(x, g, out_dtype, tm):
    m, d = x.shape
    tm = min(tm, m)
    return pl.pallas_call(
        _rmsnorm_kernel,
        out_shape=jax.ShapeDtypeStruct((m, d), out_dtype),
        grid=(m // tm,),
        in_specs=[pl.BlockSpec((tm, d), lambda i: (i, 0)),
                  pl.BlockSpec((1, d), lambda i: (0, 0))],
        out_specs=pl.BlockSpec((tm, d), lambda i: (i, 0)),
        compiler_params=_params("parallel"),
        name="rmsnorm",
    )(x, g.reshape(1, d).astype(F32))


def _mm_kernel(a_ref, b_ref, o_ref):
    o_ref[...] = jnp.dot(a_ref[...], b_ref[...], preferred_element_type=F32).astype(o_ref.dtype)


def _mm_res_kernel(a_ref, b_ref, r_ref, o_ref):
    o_ref[...] = r_ref[...] + jnp.dot(a_ref[...], b_ref[...], preferred_element_type=F32)


def _mm_cast_kernel(a_ref, b_ref, *refs):
    n_cast = (len(refs) - 1) // 2
    src_refs, o_ref, dst_refs = refs[:n_cast], refs[n_cast], refs[n_cast + 1:]
    o_ref[...] = jnp.dot(a_ref[...], b_ref[...], preferred_element_type=F32).astype(o_ref.dtype)
    for src, dst in zip(src_refs, dst_refs):
        dst[...] = src[...].astype(dst.dtype)


def _matmul(a, w, layer, out_dtype, tm, tn, res=None, casts=()):
    m, k = a.shape
    n = w.shape[2]
    tm, tn = min(tm, m), min(tn, n)
    gi, gj = m // tm, n // tn
    in_specs = [pl.BlockSpec((tm, k), lambda i, j: (i, 0)),
                pl.BlockSpec((None, k, tn), lambda i, j: (layer, 0, j))]
    args = [a, w]
    out_shape = [jax.ShapeDtypeStruct((m, n), out_dtype)]
    out_specs = [pl.BlockSpec((tm, tn), lambda i, j: (i, j))]
    kern, name = _mm_kernel, "matmul"
    if res is not None:
        assert not casts
        in_specs.append(pl.BlockSpec((tm, tn), lambda i, j: (i, j)))
        args.append(res)
        kern, name = _mm_res_kernel, "matmul_res"
    if casts:
        kern, name = _mm_cast_kernel, "matmul_cast"
    for src, src_layer, bc in casts:
        _, r, c = src.shape
        br, nj = r // gi, c // bc
        assert br * gi == r and nj * bc == c and nj <= gj
        in_specs.append(pl.BlockSpec((None, br, bc),
                                     lambda i, j, sl=src_layer, nj=nj: (sl, i, jnp.minimum(j, nj - 1))))
        args.append(src)
        out_shape.append(jax.ShapeDtypeStruct((1, r, c), BF16))
        out_specs.append(pl.BlockSpec((None, br, bc), lambda i, j, nj=nj: (0, i, jnp.minimum(j, nj - 1))))
    outs = pl.pallas_call(
        kern,
        out_shape=out_shape,
        grid=(gi, gj),
        in_specs=in_specs,
        out_specs=out_specs,
        compiler_params=_params("parallel", "arbitrary"),
        name=name,
    )(*args)
    return outs if casts else outs[0]


def _norm_mm_cast_kernel(x_ref, g_ref, b_ref, *refs, gi, nxb):
    n_cast = (len(refs) - 2) // 2
    src_refs, o_ref, dst_refs, xn_ref = refs[:n_cast], refs[n_cast], refs[n_cast + 1:-1], refs[-1]
    i, j = pl.program_id(0), pl.program_id(1)
    rb = x_ref.shape[0]

    def norm_block():
        x = x_ref[...].astype(F32)
        y = x * lax.rsqrt(jnp.mean(x * x, axis=-1, keepdims=True) + EPS)
        r0 = pl.multiple_of(jnp.minimum(j, nxb - 1) * rb, rb)
        xn_ref[i % 2, pl.ds(r0, rb), :] = (y * g_ref[...]).astype(xn_ref.dtype)

    def matmul_tile():
        o_ref[...] = jnp.dot(xn_ref[(i + 1) % 2], b_ref[...], preferred_element_type=F32).astype(o_ref.dtype)
        for src, dst in zip(src_refs, dst_refs):
            dst[...] = src[...].astype(dst.dtype)

    @pl.when(i == 0)
    def _():
        norm_block()

    @pl.when(jnp.logical_and(i > 0, i < gi))
    def _():
        matmul_tile()
        norm_block()

    @pl.when(i == gi)
    def _():
        matmul_tile()


def _norm_matmul(x, g, w, out_dtype, tm, tn, rb, casts=()):
    m, k = x.shape
    n = w.shape[2]
    gi, gj = m // tm, n // tn
    nxb = tm // rb
    assert gi * tm == m and gj * tn == n and nxb * rb == tm and nxb <= gj
    live = lambda i, v: jnp.where(i > 0, v, 0)
    in_specs = [pl.BlockSpec((rb, k), lambda i, j: (jnp.minimum(i, gi - 1) * nxb + jnp.minimum(j, nxb - 1), 0)),
                pl.BlockSpec((1, k), lambda i, j: (0, 0)),
                pl.BlockSpec((None, k, tn), lambda i, j: (0, 0, live(i, j)))]
    args = [x, g.reshape(1, k).astype(F32), w]
    out_shape = [jax.ShapeDtypeStruct((m, n), out_dtype)]
    out_specs = [pl.BlockSpec((tm, tn), lambda i, j: (jnp.maximum(i - 1, 0), live(i, j)))]
    for src, src_layer, bc in casts:
        _, r, c = src.shape
        br, nj = r // gi, c // bc
        assert br * gi == r and nj * bc == c and nj <= gj
        in_specs.append(pl.BlockSpec(
            (None, br, bc),
            lambda i, j, sl=src_layer, nj=nj: (sl, jnp.maximum(i - 1, 0), live(i, jnp.minimum(j, nj - 1)))))
        args.append(src)
        out_shape.append(jax.ShapeDtypeStruct((1, r, c), BF16))
        out_specs.append(pl.BlockSpec(
            (None, br, bc), lambda i, j, nj=nj: (0, jnp.maximum(i - 1, 0), live(i, jnp.minimum(j, nj - 1)))))
    return pl.pallas_call(
        functools.partial(_norm_mm_cast_kernel, gi=gi, nxb=nxb),
        out_shape=out_shape,
        grid=(gi + 1, gj),
        in_specs=in_specs,
        out_specs=out_specs,
        scratch_shapes=[pltpu.VMEM((2, tm, k), BF16)],
        compiler_params=_params("arbitrary", "arbitrary"),
        name="norm_matmul_cast",
    )(*args)


def _merge_kernel(ya_ref, yb_ref, yc_ref, wa_ref, wb_ref, wc_ref, ga_ref, gb_ref, gc_ref, o_ref):
    def part(y_ref, w_ref, g_ref):
        gate = _sigmoid(g_ref[...].astype(F32))
        return gate * jnp.dot(y_ref[...], w_ref[...], preferred_element_type=F32)

    o_ref[...] = (part(ya_ref, wa_ref, ga_ref) + part(yb_ref, wb_ref, gb_ref)
                  + part(yc_ref, wc_ref, gc_ref)).astype(o_ref.dtype)


def _merge(ya, yb, yc, h, w_branch, layer, gate_col0, tm, tn):
    m, d_a = ya.shape
    d_b, d_c = yb.shape[1], yc.shape[1]
    d = w_branch.shape[2]
    tm = min(tm, m)
    assert d_a == d_b and (d_a + d_b) % d_c == 0 and gate_col0 % tn == 0 and d % tn == 0
    g0, gstep = gate_col0 // tn, d // tn
    return pl.pallas_call(
        _merge_kernel,
        out_shape=jax.ShapeDtypeStruct((m, d), BF16),
        grid=(m // tm, d // tn),
        in_specs=[pl.BlockSpec((tm, d_a), lambda i, j: (i, 0)),
                  pl.BlockSpec((tm, d_b), lambda i, j: (i, 0)),
                  pl.BlockSpec((tm, d_c), lambda i, j: (i, 0)),
                  pl.BlockSpec((None, d_a, tn), lambda i, j: (layer, 0, j)),
                  pl.BlockSpec((None, d_b, tn), lambda i, j: (layer, 1, j)),
                  pl.BlockSpec((None, d_c, tn), lambda i, j: (layer, (d_a + d_b) // d_c, j)),
                  pl.BlockSpec((tm, tn), lambda i, j: (i, g0 + j)),
                  pl.BlockSpec((tm, tn), lambda i, j: (i, g0 + gstep + j)),
                  pl.BlockSpec((tm, tn), lambda i, j: (i, g0 + 2 * gstep + j))],
        out_specs=pl.BlockSpec((tm, tn), lambda i, j: (i, j)),
        compiler_params=_params("parallel", "arbitrary"),
        name="merge",
    )(ya, yb, yc, w_branch, w_branch, w_branch, h, h, h)


def _lower_bound_kernel(x_ref, o_ref):
    x = x_ref[...].astype(F32)
    e = jnp.exp(x - jnp.max(x, axis=0, keepdims=True))
    sm = e / jnp.sum(e, axis=0, keepdims=True)
    row = lax.broadcasted_iota(jnp.int32, x.shape, 0)
    acc = jnp.zeros_like(x)
    for i in range(1, x.shape[0]):
        acc = acc + jnp.where(row >= i, sm[i:i + 1, :], 0.0)
    o_ref[...] = acc


def _lower_bounds(lb_logits):
    return pl.pallas_call(
        _lower_bound_kernel,
        out_shape=jax.ShapeDtypeStruct(lb_logits.shape, F32),
        name="hgrn_lower_bound",
    )(lb_logits)


def _toeplitz_kernel(w_ref, mask_ref, o_ref, *, mult):
    rows, cols = o_ref.shape[-2:]
    p = w_ref.shape[-1]
    for j in range(o_ref.shape[1]):
        x = jnp.broadcast_to(w_ref[0, j], (rows, p))
        x = pltpu.roll(x, 0, 1, stride=1, stride_axis=0)
        o_ref[0, j] = x[:, :cols] * mult + mask_ref[j]


def _rel_bias_blocks(rel_bias, rows, cols, deltas, masks):
    p = -(-(rows + cols - 1) // LANE) * LANE
    k = np.arange(p)
    diag = np.where(k < cols, -k, p - k)
    idx = np.stack([np.clip(diag + dl, -MAX_REL, MAX_REL) + MAX_REL for dl in deltas])
    n_tab = rel_bias.shape[0] * rel_bias.shape[1]
    w = jnp.take(rel_bias.astype(F32).reshape(n_tab, -1), jnp.asarray(idx.reshape(-1)), axis=1)
    w = w.reshape(n_tab, len(deltas), 1, p)
    return pl.pallas_call(
        functools.partial(_toeplitz_kernel, mult=DH_A ** 0.5),
        out_shape=jax.ShapeDtypeStruct((n_tab, len(deltas), rows, cols), F32),
        grid=(n_tab,),
        in_specs=[pl.BlockSpec((1, len(deltas), 1, p), lambda i: (i, 0, 0, 0)),
                  pl.BlockSpec((len(deltas), rows, cols), lambda i: (0, 0, 0))],
        out_specs=pl.BlockSpec((1, len(deltas), rows, cols), lambda i: (i, 0, 0, 0)),
        compiler_params=_params("parallel"),
        name="rel_bias_blocks",
    )(w, jnp.asarray(masks, F32))


def _band_mask(q_pos, k_pos):
    qc, kc = q_pos // CHUNK, k_pos // CHUNK
    valid = (kc[None, :] <= qc[:, None]) & (kc[None, :] >= qc[:, None] - N_LEFT_CHUNKS)
    return np.where(valid, 0.0, NEG)


def _prompt_bias(rel_bias):
    r = np.arange(ATT_TQ)
    deltas = [(ATT_NKB - 1 - j) * ATT_TQ for j in range(ATT_NKB)]
    base = BAND
    masks = np.stack([_band_mask(base + r, base - dl + r) for dl in deltas])
    return _rel_bias_blocks(rel_bias, ATT_TQ, ATT_TQ, deltas, masks)


def _sample_bias(rel_bias, t, ca):
    q_pos = PAST_LEN + np.arange(t)
    k_pos = np.concatenate([PAST_LEN - ca + np.arange(ca), q_pos])
    mask = _band_mask(q_pos, k_pos) + np.where(k_pos >= 0, 0.0, NEG)[None, :]
    b = _rel_bias_blocks(rel_bias, t, ca + t, [ca], mask[None])[:, 0]
    return b[:, :, :ca], b[:, :, ca:]


def _softmax_pv(scores, values, c, ones_col):
    if all(s.shape == scores[0].shape for s in scores):
        mx = scores[0]
        for s in scores[1:]:
            mx = jnp.maximum(mx, s)
        m = jnp.max(mx, axis=-1, keepdims=True)
    else:
        m = jnp.max(scores[0], axis=-1, keepdims=True)
        for s in scores[1:]:
            m = jnp.maximum(m, jnp.max(s, axis=-1, keepdims=True))
    dh = values[0].shape[1]
    l = None
    o = None
    for s, v in zip(scores, values):
        p = jnp.exp2((s - m) * c).astype(BF16)
        if ones_col:
            v = jnp.concatenate([v, jnp.ones_like(v)], axis=1)
        else:
            ps = jnp.sum(p.astype(F32), axis=-1, keepdims=True)
            l = ps if l is None else l + ps
        pv = jnp.dot(p, v, preferred_element_type=F32)
        o = pv if o is None else o + pv
    if ones_col:
        return o[:, :dh] / o[:, dh:]
    return o / l


def _attn_prompt_kernel(q_ref, *refs, n_heads, dh, nkb, scale):
    k_refs = refs[:nkb]
    v_refs = refs[nkb:2 * nkb]
    z_ref, bias_ref, o_ref = refs[2 * nkb:]
    t = pl.program_id(1)

    def head_scores(hd):
        sl = slice(hd * dh, (hd + 1) * dh)
        q = q_ref[0, :, sl]
        scores = []
        for j in range(nkb):
            s = lax.dot_general(q, k_refs[j][0, :, sl], NT_DIMS, preferred_element_type=F32)
            s = s + bias_ref[hd, j]
            if j < nkb - 1:
                s = jnp.where(t >= nkb - 1 - j, s, NEG)
            scores.append(s)
        return scores

    scores = head_scores(0)
    for hd in range(n_heads):
        nxt = head_scores(hd + 1) if hd + 1 < n_heads else None
        sl = slice(hd * dh, (hd + 1) * dh)
        o = _softmax_pv(scores, [v_refs[j][0, :, sl] for j in range(nkb)], scale * LOG2E, True)
        z = z_ref[0, :, sl].astype(F32)
        o_ref[0, :, sl] = (o * _silu(z)).astype(o_ref.dtype)
        scores = nxt


def _attn_prompt(h3, bias, layer, d_a):
    bsz, t, _ = h3.shape
    n_heads = d_a // DH_A
    nkb = ATT_NKB
    assert t % ATT_TQ == 0 and ATT_TQ % CHUNK == 0 and BAND % ATT_TQ == 0

    def kv_spec(col, j):
        return pl.BlockSpec((1, ATT_TQ, d_a),
                            lambda b, i: (b, jnp.maximum(i - (nkb - 1 - j), 0), col))

    in_specs = ([pl.BlockSpec((1, ATT_TQ, d_a), lambda b, i: (b, i, 0))]
                + [kv_spec(1, j) for j in range(nkb)]
                + [kv_spec(2, j) for j in range(nkb)]
                + [pl.BlockSpec((1, ATT_TQ, d_a), lambda b, i: (b, i, 3)),
                   pl.BlockSpec((n_heads,) + bias.shape[1:], lambda b, i: (layer, 0, 0, 0))])
    return pl.pallas_call(
        functools.partial(_attn_prompt_kernel, n_heads=n_heads, dh=DH_A, nkb=nkb, scale=DH_A ** -0.5),
        out_shape=jax.ShapeDtypeStruct((bsz, t, d_a), BF16),
        grid=(bsz, t // ATT_TQ),
        in_specs=in_specs,
        out_specs=pl.BlockSpec((1, ATT_TQ, d_a), lambda b, i: (b, i, 0)),
        compiler_params=_params("parallel", "arbitrary"),
        name="attn_prompt",
    )(h3, *([h3] * (2 * nkb)), h3, bias)


def _attn_sample_kernel(q_ref, k_ref, v_ref, z_ref, kc_ref, vc_ref, bc_ref, bn_ref, o_ref, *, n_heads, dh, scale):
    for hd in range(n_heads):
        sl = slice(hd * dh, (hd + 1) * dh)
        q = q_ref[0, :, sl]
        s_c = lax.dot_general(q, kc_ref[0, hd].astype(BF16), NT_DIMS, preferred_element_type=F32)
        s_n = lax.dot_general(q, k_ref[0, :, sl], NT_DIMS, preferred_element_type=F32)
        scores = [s_c + bc_ref[hd], s_n + bn_ref[hd]]
        o = _softmax_pv(scores, [vc_ref[0, hd].astype(BF16), v_ref[0, :, sl]], scale * LOG2E, True)
        z = z_ref[0, :, sl].astype(F32)
        o_ref[0, :, sl] = (o * _silu(z)).astype(o_ref.dtype)


def _attn_sample(h3, k_cache, v_cache, bias_c, bias_n, layer, d_a):
    bsz, t, _ = h3.shape
    ca = k_cache.shape[3]
    n_heads = d_a // DH_A
    assert k_cache.shape[2:] == (n_heads, ca, DH_A)
    row = lambda col: pl.BlockSpec((1, t, d_a), lambda b: (b, 0, col))
    cache = pl.BlockSpec((None, 1, n_heads, ca, DH_A), lambda b: (layer, b, 0, 0, 0))
    bias = lambda a: pl.BlockSpec((n_heads,) + a.shape[1:], lambda b: (layer, 0, 0))
    return pl.pallas_call(
        functools.partial(_attn_sample_kernel, n_heads=n_heads, dh=DH_A, scale=DH_A ** -0.5),
        out_shape=jax.ShapeDtypeStruct((bsz, t, d_a), BF16),
        grid=(bsz,),
        in_specs=[row(0), row(1), row(2), row(3), cache, cache, bias(bias_c), bias(bias_n)],
        out_specs=pl.BlockSpec((1, t, d_a), lambda b: (b, 0, 0)),
        compiler_params=_params("parallel"),
        name="attn_sample",
    )(h3, h3, h3, h3, k_cache, v_cache, bias_c, bias_n)


def _mem_attn_kernel(q_ref, z_ref, mk_ref, mv_ref, o_ref, *, n_heads, dh, scale, head_axis):
    for hd in range(n_heads):
        sl = slice(hd * dh, (hd + 1) * dh)
        mk = mk_ref[0, :, hd, :] if head_axis else mk_ref[0, :, sl]
        mv = mv_ref[0, :, hd, :] if head_axis else mv_ref[0, :, sl]
        s = lax.dot_general(q_ref[0, :, sl], mk.astype(BF16), NT_DIMS, preferred_element_type=F32)
        o = _softmax_pv([s], [mv.astype(BF16)], scale * LOG2E, False)
        z = z_ref[0, :, sl].astype(F32)
        o_ref[0, :, sl] = (o * _silu(z)).astype(o_ref.dtype)


def _mem_attn(h3, mk, mv, mk_spec, mv_spec, head_axis, d_c, q_col0, tq):
    bsz, t, _ = h3.shape
    tq = min(tq, t)
    assert q_col0 % d_c == 0
    qb = q_col0 // d_c
    dh = d_c // H_C
    return pl.pallas_call(
        functools.partial(_mem_attn_kernel, n_heads=H_C, dh=dh, scale=dh ** -0.5, head_axis=head_axis),
        out_shape=jax.ShapeDtypeStruct((bsz, t, d_c), BF16),
        grid=(bsz, t // tq),
        in_specs=[pl.BlockSpec((1, tq, d_c), lambda b, i: (b, i, qb)),
                  pl.BlockSpec((1, tq, d_c), lambda b, i: (b, i, qb + 1)),
                  mk_spec, mv_spec],
        out_specs=pl.BlockSpec((1, tq, d_c), lambda b, i: (b, i, 0)),
        compiler_params=_params("parallel", "arbitrary"),
        name="mem_attn",
    )(h3, h3, mk, mv)


def _seg_cumsum(x, row, group):
    d = 1
    while d < group:
        x = x + jnp.where((row & (group - 1)) >= d, pltpu.roll(x, d, 0), 0.0)
        d *= 2
    return x


def _hgrn_masks():
    t = np.arange(HG_L)[:, None]
    s = np.arange(HG_L)[None, :]
    m16 = [(t // 16 == i) & (s < 16 * i) for i in range(1, 4)]
    m4 = [(t // 16 == s // 16) & ((t % 16) // 4 == i) & (s % 16 < 4 * i) for i in range(1, 4)]
    base = (t // 4 == s // 4) & (s <= t)
    m1 = np.concatenate(m16, axis=1).astype(np.float32)
    m2 = np.concatenate(m4 + [base], axis=1).astype(np.float32)
    return jnp.asarray(m1, BF16), jnp.asarray(m2, BF16)


def _bcast_rows(ref, offsets, reps):
    return jnp.concatenate(
        [jnp.broadcast_to(ref[o:o + 1, :], (reps, LANE)) for o in offsets], axis=0)


def _hgrn_prep(q, g, k, v, cs_ref):
    L = HG_L
    n = q.shape[0] // L
    starts = [c * L for c in range(n)]
    row = lax.broadcasted_iota(jnp.int32, q.shape, 0)
    c4 = _seg_cumsum(g, row, 4)
    c16 = _seg_cumsum(g, row, 16)
    c16_ref, b_ref, carry_ref = cs_ref.at[0], cs_ref.at[1], cs_ref.at[2]
    c16_ref[...] = c16
    ng = L // 16
    for c, s in enumerate(starts):
        tot = jnp.zeros((1, LANE), F32)
        for j in range(ng):
            carry_ref[c * ng + j:c * ng + j + 1, :] = tot
            if j + 1 < ng:
                tot = tot + c16_ref[s + 16 * j + 15:s + 16 * j + 16, :]
    b = c16 + _bcast_rows(carry_ref, list(range(n * ng)), 16)
    b_ref[...] = b

    k16 = []
    for i in range(1, L // 16):
        ref_rows = _bcast_rows(b_ref, [s + 16 * i - 1 for s in starts], L)
        k16.append((k * jnp.exp2(jnp.minimum(ref_rows - b, 0.0))).astype(BF16))
    k4 = []
    for i in range(1, 4):
        ref_rows = _bcast_rows(c16_ref, [s + 16 * j + 4 * i - 1 for s in starts for j in range(L // 16)], 16)
        k4.append((k * jnp.exp2(jnp.minimum(ref_rows - c16, 0.0))).astype(BF16))
    k4.append((k * jnp.exp2(-c4)).astype(BF16))
    return dict(
        starts=starts, b_ref=b_ref, k16=k16, k4=k4, vb=v.astype(BF16),
        q4=(q * jnp.exp2(c4)).astype(BF16), q16=(q * jnp.exp2(c16)).astype(BF16),
        q64=(q * jnp.exp2(b)).astype(BF16),
        kd=(k * jnp.exp2(_bcast_rows(b_ref, [s + L - 1 for s in starts], L) - b)).astype(BF16))


def _hgrn_scores(p, m1, m2):
    scores = []
    for s in p["starts"]:
        sl = slice(s, s + HG_L)
        a1 = lax.dot_general(p["q16"][sl], jnp.concatenate([x[sl] for x in p["k16"]], axis=0), NT_DIMS,
                             preferred_element_type=F32)
        a2 = lax.dot_general(p["q4"][sl], jnp.concatenate([x[sl] for x in p["k4"]], axis=0), NT_DIMS,
                             preferred_element_type=F32)
        scores.append((a1.astype(BF16) * m1, a2.astype(BF16) * m2))
    return scores


def _hgrn_outputs(p, scores, sts, chain_head):
    L = HG_L
    sts = list(sts)
    outs = []
    for c, s in enumerate(p["starts"]):
        sl = slice(s, s + L)
        a1, a2 = scores[c]
        st = sts[chain_head[c]]
        vc = p["vb"][sl]
        outs.append(
            jnp.dot(a1, jnp.concatenate([vc] * 3, axis=0), preferred_element_type=F32)
            + jnp.dot(a2, jnp.concatenate([vc] * 4, axis=0), preferred_element_type=F32)
            + lax.dot_general(p["q64"][sl], st.astype(BF16), NT_DIMS, preferred_element_type=F32))
        sts[chain_head[c]] = (st * jnp.exp2(p["b_ref"][s + L - 1:s + L, :])
                              + lax.dot_general(vc, p["kd"][sl], TN_DIMS, preferred_element_type=F32))
    return jnp.concatenate(outs, axis=0), sts


def _hgrn_unit(q_ref, f_ref, i_ref, z_ref, lb_ref, gn_ref, m1_ref, m2_ref, y_ref, st_ref, cs_ref,
               *, rows, n_sub, unroll):
    m1 = m1_ref[...]
    m2 = m2_ref[...]
    chains = [(u, p) for u in range(unroll) for p in range(HG_HP)]
    chain_head = [p for _, p in chains]
    lb = jnp.concatenate([jnp.broadcast_to(lb_ref[p], (HG_L, LANE)) for _, p in chains], axis=0)
    gn = jnp.concatenate([jnp.broadcast_to(gn_ref[p], (HG_L, LANE)) for _, p in chains], axis=0)

    def stack(ref, r0s):
        parts = []
        for u, p in chains:
            x = ref[0, r0s[u]:r0s[u] + rows, p * LANE:(p + 1) * LANE].astype(F32)
            if rows < HG_L:
                x = jnp.concatenate([x, jnp.zeros((HG_L - rows, LANE), F32)], axis=0)
            parts.append(x)
        return jnp.concatenate(parts, axis=0)

    def finish(pending, sts):
        p, scores, zr, r0s = pending
        o, sts = _hgrn_outputs(p, scores, sts, chain_head)
        on = o * lax.rsqrt(jnp.mean(o * o, axis=-1, keepdims=True) + EPS) * gn
        y = (on * _silu(zr)).astype(y_ref.dtype)
        for c, (u, hp) in enumerate(chains):
            y_ref[0, r0s[u]:r0s[u] + rows, hp * LANE:(hp + 1) * LANE] = y[c * HG_L:c * HG_L + rows]
        return sts

    sts = [st_ref[p] for p in range(HG_HP)]
    pending = None
    for gi in range(n_sub // unroll):
        r0s = [(gi * unroll + u) * rows for u in range(unroll)]
        qr, fr, vr, zr = (stack(ref, r0s) for ref in (q_ref, f_ref, i_ref, z_ref))
        kf = (1.0 - lb) * _sigmoid(-fr)
        g = jnp.log2(1.0 - kf)
        if rows < HG_L:
            live = (lax.broadcasted_iota(jnp.int32, fr.shape, 0) & (HG_L - 1)) < rows
            kf = jnp.where(live, kf, 0.0)
            g = jnp.where(live, g, 0.0)
        p = _hgrn_prep(_silu(qr), g, kf, vr, cs_ref.at[gi])
        yield
        scores = _hgrn_scores(p, m1, m2)
        if pending is not None:
            sts = finish(pending, sts)
        pending = (p, scores, zr, r0s)
        yield
    sts = finish(pending, sts)
    for p in range(HG_HP):
        st_ref[p] = sts[p]


def _hgrn_kernel(q_ref, f_ref, i_ref, z_ref, lb_ref, gn_ref, s0_ref, m1_ref, m2_ref,
                 y_ref, s_out_ref, st_ref, cs_ref, *, rows, n_sub, unroll):
    tb = pl.program_id(2)

    @pl.when(tb == 0)
    def _():
        for p in range(HG_HP):
            st_ref[p] = s0_ref[0, p].astype(F32).T

    for _ in _hgrn_unit(q_ref, f_ref, i_ref, z_ref, lb_ref, gn_ref, m1_ref, m2_ref, y_ref, st_ref, cs_ref,
                        rows=rows, n_sub=n_sub, unroll=unroll):
        pass

    @pl.when(tb == pl.num_programs(2) - 1)
    def _():
        for p in range(HG_HP):
            s_out_ref[0, p] = st_ref[p].T


def _hgrn(h3, lb, gain, s0, layer, masks, col0, d_b):
    bsz, t, _ = h3.shape
    n_heads = d_b // DK_B
    assert col0 % (HG_HP * LANE) == 0 and n_heads % HG_HP == 0 and DK_B == LANE
    assert s0.shape[1:] == (bsz, n_heads, DK_B, LANE)
    if t % HG_L == 0:
        rows, tb = HG_L, min(HG_TB, t)
    else:
        assert t < HG_L and t % 16 == 0
        rows, tb = t, t
    n_sub = tb // rows
    unroll = HG_UNROLL if n_sub % HG_UNROLL == 0 else 1
    assert t % tb == 0
    w = HG_HP * LANE
    c0 = col0 // w
    col = lambda seg: pl.BlockSpec((1, tb, w), lambda b, hp, i: (b, i, c0 + seg * (n_heads // HG_HP) + hp))
    per_head = pl.BlockSpec((HG_HP, 1, LANE), lambda b, hp, i: (hp, 0, 0))
    m1, m2 = masks
    y, s_new = pl.pallas_call(
        functools.partial(_hgrn_kernel, rows=rows, n_sub=n_sub, unroll=unroll),
        out_shape=(jax.ShapeDtypeStruct((bsz, t, d_b), BF16),
                   jax.ShapeDtypeStruct(s0.shape[1:], F32)),
        grid=(bsz, n_heads // HG_HP, t // tb),
        in_specs=[col(0), col(1), col(2), col(3), per_head, per_head,
                  pl.BlockSpec((None, 1, HG_HP, DK_B, LANE), lambda b, hp, i: (layer, b, hp, 0, 0)),
                  pl.BlockSpec(m1.shape, lambda b, hp, i: (0, 0)),
                  pl.BlockSpec(m2.shape, lambda b, hp, i: (0, 0))],
        out_specs=(pl.BlockSpec((1, tb, w), lambda b, hp, i: (b, i, hp)),
                   pl.BlockSpec((1, HG_HP, DK_B, LANE), lambda b, hp, i: (b, hp, 0, 0))),
        scratch_shapes=[pltpu.VMEM((HG_HP, LANE, DK_B), F32),
                        pltpu.VMEM((n_sub // unroll, 3, unroll * HG_HP * HG_L, LANE), F32)],
        compiler_params=_params("parallel", "parallel", "arbitrary"),
        name="hgrn2",
    )(h3, h3, h3, h3, lb.reshape(n_heads, 1, DK_B), gain.reshape(n_heads, 1, LANE).astype(F32),
      s0, m1, m2)
    return y, s_new


def _kv_tail_kernel(k_ref, v_ref, ko_ref, vo_ref):
    ko_ref[0, 0] = k_ref[0].astype(ko_ref.dtype)
    vo_ref[0, 0] = v_ref[0].astype(vo_ref.dtype)


def _kv_tail(h3, d_a, rows):
    bsz, t, _ = h3.shape
    n_heads = d_a // DH_A
    assert t % rows == 0
    last = t // rows - 1
    col = lambda seg: pl.BlockSpec((1, rows, DH_A), lambda b, hd: (b, last, seg * n_heads + hd))
    out = pl.BlockSpec((1, 1, rows, DH_A), lambda b, hd: (b, hd, 0, 0))
    return pl.pallas_call(
        _kv_tail_kernel,
        out_shape=[jax.ShapeDtypeStruct((bsz, n_heads, rows, DH_A), F32)] * 2,
        grid=(bsz, n_heads),
        in_specs=[col(1), col(2)],
        out_specs=[out, out],
        compiler_params=_params("parallel", "parallel"),
        name="kv_tail",
    )(h3, h3)


def _layer(x, attend, s0, s0_layer, mem_args, norm_g, w_in, lb, hgrn_g, w_branch, w_out, masks, dims, casts=(),
           cache_rows=None):
    d_a, d_b, d_c = dims
    bsz, t, d = x.shape
    m = bsz * t
    x2 = x.reshape(m, d)
    cast_out = ()
    if casts:
        h, *cast_out = _norm_matmul(x2, norm_g, w_in, BF16, MM_TM, MM_TN, NORM_RB, casts=casts)
        w_branch, w_out = cast_out[:2]
    else:
        h = _matmul(_rmsnorm(x2, norm_g, BF16, NORM_TM), w_in, 0, BF16, MM_TM, MM_TN)
    h3 = h.reshape(bsz, t, -1)
    ya = attend(h3)
    yb, s_new = _hgrn(h3, lb, hgrn_g, s0, s0_layer, masks, 4 * d_a, d_b)
    yc = _mem_attn(h3, *mem_args, d_c, 4 * d_a + 4 * d_b, MEM_TQ)
    merged = _merge(ya.reshape(m, d_a), yb.reshape(m, d_b), yc.reshape(m, d_c), h, w_branch, 0,
                    4 * d_a + 4 * d_b + 2 * d_c, MM_TM, MM_TN)
    x_new = _matmul(merged, w_out, 0, F32, MM_TM, MM_TN, res=x2).reshape(bsz, t, d)
    ka, va = _kv_tail(h3, d_a, cache_rows)
    return x_new, ka, va, s_new, tuple(cast_out)


def kernel(x_prompt, x_sample, mem_prompt, cache_attn_k, cache_attn_v, state_hgrn, cache_mem_k, cache_mem_v, norm_gain, w_in, rel_bias, lb_logits, hgrn_norm_gain, mem_norm_gain, w_mem_kv, w_branch, w_out, final_norm_gain):
    depth = w_in.shape[0]
    bp, tp, d = x_prompt.shape
    bs, ts, _ = x_sample.shape
    h_a = rel_bias.shape[1]
    d_a = h_a * DH_A
    d_b = lb_logits.shape[1]
    d_c = w_mem_kv.shape[2] // 2
    h_b = d_b // DK_B
    n_mem = mem_prompt.shape[1]
    dims = (d_a, d_b, d_c)
    ca_s = cache_attn_k.shape[2]
    ca_p = min(BAND, tp)

    w_in_l = w_in[:1].astype(BF16)
    w_mem_l = w_mem_kv[:1].astype(BF16)

    kc_s = jnp.swapaxes(cache_attn_k, 2, 3)
    vc_s = jnp.swapaxes(cache_attn_v, 2, 3)

    lb_all = _lower_bounds(lb_logits)
    masks = _hgrn_masks()
    bias_p = _prompt_bias(rel_bias)
    bias_c, bias_n = _sample_bias(rel_bias, ts, ca_s)
    s0_p = jnp.zeros((1, bp, h_b, DK_B, d_b // h_b), F32)
    mem2 = mem_prompt.reshape(bp * n_mem, d)

    xp, xs = x_prompt, x_sample
    kp_l, vp_l, sp_l, mkp_l, mvp_l, ks_l, vs_l, ss_l = [], [], [], [], [], [], [], []
    for l in range(depth):
        casts = [(w_branch, l, CAST_COLS), (w_out, l, CAST_COLS)]
        if l + 1 < depth:
            casts += [(w_in, l + 1, MM_TN), (w_mem_kv, l + 1, CAST_COLS // 2)]

        memn = _rmsnorm(mem2, mem_norm_gain[l], BF16, NORM_TM)
        mkv = _matmul(memn, w_mem_l, 0, F32, MM_TM, MM_TN).reshape(bp, n_mem, 2 * d_c)
        mem_p = (mkv, mkv,
                 pl.BlockSpec((1, n_mem, d_c), lambda b, i: (b, 0, 0)),
                 pl.BlockSpec((1, n_mem, d_c), lambda b, i: (b, 0, 1)), False)
        att_p = lambda h3: _attn_prompt(h3, bias_p, l, d_a)
        xp, ka, va, s_fin, cast_out = _layer(xp, att_p, s0_p, 0, mem_p, norm_gain[l], w_in_l, lb_all[l],
                                             hgrn_norm_gain[l], None, None, masks, dims, casts=casts,
                                             cache_rows=ca_p)
        w_branch_l, w_out_l = cast_out[:2]
        kp_l.append(ka)
        vp_l.append(va)
        sp_l.append(s_fin)
        mkp_l.append(mkv[:, :, :d_c].reshape(bp, n_mem, H_C, d_c // H_C))
        mvp_l.append(mkv[:, :, d_c:].reshape(bp, n_mem, H_C, d_c // H_C))

        mem_blk = pl.BlockSpec((None, 1, n_mem, H_C, d_c // H_C), lambda b, i: (l, b, 0, 0, 0))
        mem_s = (cache_mem_k, cache_mem_v, mem_blk, mem_blk, True)
        att_s = lambda h3: _attn_sample(h3, kc_s, vc_s, bias_c, bias_n, l, d_a)
        xs, ka_s, va_s, s_new, _ = _layer(xs, att_s, state_hgrn, l, mem_s, norm_gain[l], w_in_l, lb_all[l],
                                          hgrn_norm_gain[l], w_branch_l, w_out_l, masks, dims, cache_rows=ts)
        if l + 1 < depth:
            w_in_l, w_mem_l = cast_out[2:4]
        ks_l.append(ka_s)
        vs_l.append(va_s)
        ss_l.append(s_new)

    y_prompt = _rmsnorm(xp.reshape(bp * tp, d), final_norm_gain, F32, NORM_TM).reshape(bp, tp, d)
    y_sample = _rmsnorm(xs.reshape(bs * ts, d), final_norm_gain, F32, NORM_TM).reshape(bs, ts, d)
    tail = lambda parts: jnp.stack(parts).swapaxes(2, 3)
    return (y_prompt, y_sample,
            tail(kp_l), tail(vp_l), jnp.stack(sp_l), jnp.stack(mkp_l), jnp.stack(mvp_l),
            tail(ks_l), tail(vs_l), jnp.stack(ss_l))
```

```python
import functools

import numpy as np
import jax
import jax.numpy as jnp
from jax import lax
from jax.experimental import pallas as pl
from jax.experimental.pallas import tpu as pltpu

F32 = jnp.float32
BF16 = jnp.bfloat16

PAST_LEN = 4096
CHUNK = 64
N_LEFT_CHUNKS = 8
BAND = N_LEFT_CHUNKS * CHUNK
DH_A = 128
MAX_REL = 256
DK_B = 128
H_C = 4
EPS = 1e-6
NEG = -1e30
LOG2E = 1.4426950408889634

LANE = 128
VMEM_LIMIT_BYTES = 56 * 1024 * 1024

MM_TM = 1024
MM_TN = 1024
NORM_TM = 256
NORM_RB = 64
CAST_COLS = 256
MEM_TQ = 1024
ATT_TQ = 256
ATT_RQ = 128
ATT_NKB = BAND // ATT_TQ + 1
HG_L = 64
HG_TB = 2048
HG_HP = 4
HG_UNROLL = 4

NT_DIMS = (((1,), (1,)), ((), ()))
TN_DIMS = (((0,), (0,)), ((), ()))


def _params(*sem):
    return pltpu.CompilerParams(dimension_semantics=sem, vmem_limit_bytes=VMEM_LIMIT_BYTES)


def _sigmoid(x):
    return 0.5 * jnp.tanh(0.5 * x) + 0.5


def _silu(z):
    return z * _sigmoid(z)


def _rmsnorm_kernel(x_ref, g_ref, o_ref):
    x = x_ref[...].astype(F32)
    y = x * lax.rsqrt(jnp.mean(x * x, axis=-1, keepdims=True) + EPS)
    o_ref[...] = (y * g_ref[...]).astype(o_ref.dtype)


def _rmsnorm(x, g, out_dtype, tm):
    m, d = x.shape
    tm = min(tm, m)
    return pl.pallas_call(
        _rmsnorm_kernel,
        out_shape=jax.ShapeDtypeStruct((m, d), out_dtype),
        grid=(m // tm,),
        in_specs=[pl.BlockSpec((tm, d), lambda i: (i, 0)),
                  pl.BlockSpec((1, d), lambda i: (0, 0))],
        out_specs=pl.BlockSpec((tm, d), lambda i: (i, 0)),
        compiler_params=_params("parallel"),
        name="rmsnorm",
    )(x, g.reshape(1, d).astype(F32))


def _mm_kernel(a_ref, b_ref, o_ref):
    o_ref[...] = jnp.dot(a_ref[...], b_ref[...], preferred_element_type=F32).astype(o_ref.dtype)


def _mm_res_kernel(a_ref, b_ref, r_ref, o_ref):
    o_ref[...] = r_ref[...] + jnp.dot(a_ref[...], b_ref[...], preferred_element_type=F32)


def _mm_cast_kernel(a_ref, b_ref, *refs):
    n_cast = (len(refs) - 1) // 2
    src_refs, o_ref, dst_refs = refs[:n_cast], refs[n_cast], refs[n_cast + 1:]
    o_ref[...] = jnp.dot(a_ref[...], b_ref[...], preferred_element_type=F32).astype(o_ref.dtype)
    for src, dst in zip(src_refs, dst_refs):
        dst[...] = src[...].astype(dst.dtype)


def _matmul(a, w, layer, out_dtype, tm, tn, res=None, casts=()):
    m, k = a.shape
    n = w.shape[2]
    tm, tn = min(tm, m), min(tn, n)
    gi, gj = m // tm, n // tn
    in_specs = [pl.BlockSpec((tm, k), lambda i, j: (i, 0)),
                pl.BlockSpec((None, k, tn), lambda i, j: (layer, 0, j))]
    args = [a, w]
    out_shape = [jax.ShapeDtypeStruct((m, n), out_dtype)]
    out_specs = [pl.BlockSpec((tm, tn), lambda i, j: (i, j))]
    kern, name = _mm_kernel, "matmul"
    if res is not None:
        assert not casts
        in_specs.append(pl.BlockSpec((tm, tn), lambda i, j: (i, j)))
        args.append(res)
        kern, name = _mm_res_kernel, "matmul_res"
    if casts:
        kern, name = _mm_cast_kernel, "matmul_cast"
    for src, src_layer, bc in casts:
        _, r, c = src.shape
        br, nj = r // gi, c // bc
        assert br * gi == r and nj * bc == c and nj <= gj
        in_specs.append(pl.BlockSpec((None, br, bc),
                                     lambda i, j, sl=src_layer, nj=nj: (sl, i, jnp.minimum(j, nj - 1))))
        args.append(src)
        out_shape.append(jax.ShapeDtypeStruct((1, r, c), BF16))
        out_specs.append(pl.BlockSpec((None, br, bc), lambda i, j, nj=nj: (0, i, jnp.minimum(j, nj - 1))))
    outs = pl.pallas_call(
        kern,
        out_shape=out_shape,
        grid=(gi, gj),
        in_specs=in_specs,
        out_specs=out_specs,
        compiler_params=_params("parallel", "arbitrary"),
        name=name,
    )(*args)
    return outs if casts else outs[0]


def _norm_mm_cast_kernel(x_ref, g_ref, b_ref, *refs, gi, nxb):
    n_cast = (len(refs) - 2) // 2
    src_refs, o_ref, dst_refs, xn_ref = refs[:n_cast], refs[n_cast], refs[n_cast + 1:-1], refs[-1]
    i, j = pl.program_id(0), pl.program_id(1)
    rb = x_ref.shape[0]

    def norm_block():
        x = x_ref[...].astype(F32)
        y = x * lax.rsqrt(jnp.mean(x * x, axis=-1, keepdims=True) + EPS)
        r0 = pl.multiple_of(jnp.minimum(j, nxb - 1) * rb, rb)
        xn_ref[i % 2, pl.ds(r0, rb), :] = (y * g_ref[...]).astype(xn_ref.dtype)

    def matmul_tile():
        o_ref[...] = jnp.dot(xn_ref[(i + 1) % 2], b_ref[...], preferred_element_type=F32).astype(o_ref.dtype)
        for src, dst in zip(src_refs, dst_refs):
            dst[...] = src[...].astype(dst.dtype)

    @pl.when(i == 0)
    def _():
        norm_block()

    @pl.when(jnp.logical_and(i > 0, i < gi))
    def _():
        matmul_tile()
        norm_block()

    @pl.when(i == gi)
    def _():
        matmul_tile()


def _norm_matmul(x, g, w, out_dtype, tm, tn, rb, casts=()):
    m, k = x.shape
    n = w.shape[2]
    gi, gj = m // tm, n // tn
    nxb = tm // rb
    assert gi * tm == m and gj * tn == n and nxb * rb == tm and nxb <= gj
    live = lambda i, v: jnp.where(i > 0, v, 0)
    in_specs = [pl.BlockSpec((rb, k), lambda i, j: (jnp.minimum(i, gi - 1) * nxb + jnp.minimum(j, nxb - 1), 0)),
                pl.BlockSpec((1, k), lambda i, j: (0, 0)),
                pl.BlockSpec((None, k, tn), lambda i, j: (0, 0, live(i, j)))]
    args = [x, g.reshape(1, k).astype(F32), w]
    out_shape = [jax.ShapeDtypeStruct((m, n), out_dtype)]
    out_specs = [pl.BlockSpec((tm, tn), lambda i, j: (jnp.maximum(i - 1, 0), live(i, j)))]
    for src, src_layer, bc in casts:
        _, r, c = src.shape
        br, nj = r // gi, c // bc
        assert br * gi == r and nj * bc == c and nj <= gj
        in_specs.append(pl.BlockSpec(
            (None, br, bc),
            lambda i, j, sl=src_layer, nj=nj: (sl, jnp.maximum(i - 1, 0), live(i, jnp.minimum(j, nj - 1)))))
        args.append(src)
        out_shape.append(jax.ShapeDtypeStruct((1, r, c), BF16))
        out_specs.append(pl.BlockSpec(
            (None, br, bc), lambda i, j, nj=nj: (0, jnp.maximum(i - 1, 0), live(i, jnp.minimum(j, nj - 1)))))
    return pl.pallas_call(
        functools.partial(_norm_mm_cast_kernel, gi=gi, nxb=nxb),
        out_shape=out_shape,
        grid=(gi + 1, gj),
        in_specs=in_specs,
        out_specs=out_specs,
        scratch_shapes=[pltpu.VMEM((2, tm, k), BF16)],
        compiler_params=_params("arbitrary", "arbitrary"),
        name="norm_matmul_cast",
    )(*args)


def _merge_kernel(ya_ref, yb_ref, yc_ref, wa_ref, wb_ref, wc_ref, ga_ref, gb_ref, gc_ref, o_ref):
    def part(y_ref, w_ref, g_ref):
        gate = _sigmoid(g_ref[...].astype(F32))
        return gate * jnp.dot(y_ref[...], w_ref[...], preferred_element_type=F32)

    o_ref[...] = (part(ya_ref, wa_ref, ga_ref) + part(yb_ref, wb_ref, gb_ref)
                  + part(yc_ref, wc_ref, gc_ref)).astype(o_ref.dtype)


def _merge(ya, yb, yc, h, w_branch, layer, gate_col0, tm, tn):
    m, d_a = ya.shape
    d_b, d_c = yb.shape[1], yc.shape[1]
    d = w_branch.shape[2]
    tm = min(tm, m)
    assert d_a == d_b and (d_a + d_b) % d_c == 0 and gate_col0 % tn == 0 and d % tn == 0
    g0, gstep = gate_col0 // tn, d // tn
    return pl.pallas_call(
        _merge_kernel,
        out_shape=jax.ShapeDtypeStruct((m, d), BF16),
        grid=(m // tm, d // tn),
        in_specs=[pl.BlockSpec((tm, d_a), lambda i, j: (i, 0)),
                  pl.BlockSpec((tm, d_b), lambda i, j: (i, 0)),
                  pl.BlockSpec((tm, d_c), lambda i, j: (i, 0)),
                  pl.BlockSpec((None, d_a, tn), lambda i, j: (layer, 0, j)),
                  pl.BlockSpec((None, d_b, tn), lambda i, j: (layer, 1, j)),
                  pl.BlockSpec((None, d_c, tn), lambda i, j: (layer, (d_a + d_b) // d_c, j)),
                  pl.BlockSpec((tm, tn), lambda i, j: (i, g0 + j)),
                  pl.BlockSpec((tm, tn), lambda i, j: (i, g0 + gstep + j)),
                  pl.BlockSpec((tm, tn), lambda i, j: (i, g0 + 2 * gstep + j))],
        out_specs=pl.BlockSpec((tm, tn), lambda i, j: (i, j)),
        compiler_params=_params("parallel", "arbitrary"),
        name="merge",
    )(ya, yb, yc, w_branch, w_branch, w_branch, h, h, h)


def _lower_bound_kernel(x_ref, o_ref):
    x = x_ref[...].astype(F32)
    e = jnp.exp(x - jnp.max(x, axis=0, keepdims=True))
    sm = e / jnp.sum(e, axis=0, keepdims=True)
    row = lax.broadcasted_iota(jnp.int32, x.shape, 0)
    acc = jnp.zeros_like(x)
    for i in range(1, x.shape[0]):
        acc = acc + jnp.where(row >= i, sm[i:i + 1, :], 0.0)
    o_ref[...] = acc


def _lower_bounds(lb_logits):
    return pl.pallas_call(
        _lower_bound_kernel,
        out_shape=jax.ShapeDtypeStruct(lb_logits.shape, F32),
        name="hgrn_lower_bound",
    )(lb_logits)


def _toeplitz_kernel(w_ref, mask_ref, o_ref, *, mult):
    rows, cols = o_ref.shape[-2:]
    p = w_ref.shape[-1]
    for j in range(o_ref.shape[1]):
        x = jnp.broadcast_to(w_ref[0, j], (rows, p))
        x = pltpu.roll(x, 0, 1, stride=1, stride_axis=0)
        o_ref[0, j] = x[:, :cols] * mult + mask_ref[j]


def _rel_bias_blocks(rel_bias, rows, cols, deltas, masks):
    p = -(-(rows + cols - 1) // LANE) * LANE
    k = np.arange(p)
    diag = np.where(k < cols, -k, p - k)
    idx = np.stack([np.clip(diag + dl, -MAX_REL, MAX_REL) + MAX_REL for dl in deltas])
    n_tab = rel_bias.shape[0] * rel_bias.shape[1]
    w = jnp.take(rel_bias.astype(F32).reshape(n_tab, -1), jnp.asarray(idx.reshape(-1)), axis=1)
    w = w.reshape(n_tab, len(deltas), 1, p)
    return pl.pallas_call(
        functools.partial(_toeplitz_kernel, mult=DH_A ** 0.5),
        out_shape=jax.ShapeDtypeStruct((n_tab, len(deltas), rows, cols), F32),
        grid=(n_tab,),
        in_specs=[pl.BlockSpec((1, len(deltas), 1, p), lambda i: (i, 0, 0, 0)),
                  pl.BlockSpec((len(deltas), rows, cols), lambda i: (0, 0, 0))],
        out_specs=pl.BlockSpec((1, len(deltas), rows, cols), lambda i: (i, 0, 0, 0)),
        compiler_params=_params("parallel"),
        name="rel_bias_blocks",
    )(w, jnp.asarray(masks, F32))


def _band_mask(q_pos, k_pos):
    qc, kc = q_pos // CHUNK, k_pos // CHUNK
    valid = (kc[None, :] <= qc[:, None]) & (kc[None, :] >= qc[:, None] - N_LEFT_CHUNKS)
    return np.where(valid, 0.0, NEG)


def _prompt_bias(rel_bias):
    r = np.arange(ATT_TQ)
    deltas = [(ATT_NKB - 1 - j) * ATT_TQ for j in range(ATT_NKB)]
    base = BAND
    masks = np.stack([_band_mask(base + r, base - dl + r) for dl in deltas])
    return _rel_bias_blocks(rel_bias, ATT_TQ, ATT_TQ, deltas, masks)


def _sample_bias(rel_bias, t, ca):
    q_pos = PAST_LEN + np.arange(t)
    k_pos = np.concatenate([PAST_LEN - ca + np.arange(ca), q_pos])
    mask = _band_mask(q_pos, k_pos) + np.where(k_pos >= 0, 0.0, NEG)[None, :]
    b = _rel_bias_blocks(rel_bias, t, ca + t, [ca], mask[None])[:, 0]
    return b[:, :, :ca], b[:, :, ca:]


def _softmax_pv(scores, values, c, ones_col):
    if all(s.shape == scores[0].shape for s in scores):
        mx = scores[0]
        for s in scores[1:]:
            mx = jnp.maximum(mx, s)
        m = jnp.max(mx, axis=-1, keepdims=True)
    else:
        m = jnp.max(scores[0], axis=-1, keepdims=True)
        for s in scores[1:]:
            m = jnp.maximum(m, jnp.max(s, axis=-1, keepdims=True))
    dh = values[0].shape[1]
    l = None
    o = None
    for s, v in zip(scores, values):
        p = jnp.exp2((s - m) * c).astype(BF16)
        if ones_col:
            v = jnp.concatenate([v, jnp.ones_like(v)], axis=1)
        else:
            ps = jnp.sum(p.astype(F32), axis=-1, keepdims=True)
            l = ps if l is None else l + ps
        pv = jnp.dot(p, v, preferred_element_type=F32)
        o = pv if o is None else o + pv
    if ones_col:
        return o[:, :dh] / o[:, dh:]
    return o / l


def _attn_prompt_kernel(q_ref, *refs, n_heads, dh, nkb, scale):
    k_refs = refs[:nkb]
    v_refs = refs[nkb:2 * nkb]
    z_ref, bias_ref, o_ref = refs[2 * nkb:]
    t = pl.program_id(1)

    tq = q_ref.shape[1]
    rq = ATT_RQ

    def key_cols(r0, j):
        lo = r0 if j == 0 else 0
        hi = r0 + rq if j == nkb - 1 else tq
        return slice(lo, hi)

    for hd in range(n_heads):
        sl = slice(hd * dh, (hd + 1) * dh)
        for r0 in range(0, tq, rq):
            rows = slice(r0, r0 + rq)
            q = q_ref[0, rows, sl]
            scores, values = [], []
            for j in range(nkb):
                kc = key_cols(r0, j)
                s = lax.dot_general(q, k_refs[j][0, kc, sl], NT_DIMS, preferred_element_type=F32)
                s = s + bias_ref[hd, j, rows, kc]
                if j < nkb - 1:
                    s = jnp.where(t >= nkb - 1 - j, s, NEG)
                scores.append(s)
                values.append(v_refs[j][0, kc, sl])
            o = _softmax_pv(scores, values, scale * LOG2E, True)
            z = z_ref[0, rows, sl].astype(F32)
            o_ref[0, rows, sl] = (o * _silu(z)).astype(o_ref.dtype)


def _attn_prompt(h3, bias, layer, d_a):
    bsz, t, _ = h3.shape
    n_heads = d_a // DH_A
    nkb = ATT_NKB
    assert t % ATT_TQ == 0 and ATT_TQ % CHUNK == 0 and BAND % ATT_TQ == 0

    def kv_spec(col, j):
        return pl.BlockSpec((1, ATT_TQ, d_a),
                            lambda b, i: (b, jnp.maximum(i - (nkb - 1 - j), 0), col))

    in_specs = ([pl.BlockSpec((1, ATT_TQ, d_a), lambda b, i: (b, i, 0))]
                + [kv_spec(1, j) for j in range(nkb)]
                + [kv_spec(2, j) for j in range(nkb)]
                + [pl.BlockSpec((1, ATT_TQ, d_a), lambda b, i: (b, i, 3)),
                   pl.BlockSpec((n_heads,) + bias.shape[1:], lambda b, i: (layer, 0, 0, 0))])
    return pl.pallas_call(
        functools.partial(_attn_prompt_kernel, n_heads=n_heads, dh=DH_A, nkb=nkb, scale=DH_A ** -0.5),
        out_shape=jax.ShapeDtypeStruct((bsz, t, d_a), BF16),
        grid=(bsz, t // ATT_TQ),
        in_specs=in_specs,
        out_specs=pl.BlockSpec((1, ATT_TQ, d_a), lambda b, i: (b, i, 0)),
        compiler_params=_params("parallel", "arbitrary"),
        name="attn_prompt",
    )(h3, *([h3] * (2 * nkb)), h3, bias)


def _attn_sample_kernel(q_ref, k_ref, v_ref, z_ref, kc_ref, vc_ref, bc_ref, bn_ref, o_ref, *, n_heads, dh, scale):
    for hd in range(n_heads):
        sl = slice(hd * dh, (hd + 1) * dh)
        q = q_ref[0, :, sl]
        s_c = lax.dot_general(q, kc_ref[0, hd].astype(BF16), NT_DIMS, preferred_element_type=F32)
        s_n = lax.dot_general(q, k_ref[0, :, sl], NT_DIMS, preferred_element_type=F32)
        scores = [s_c + bc_ref[hd], s_n + bn_ref[hd]]
        o = _softmax_pv(scores, [vc_ref[0, hd].astype(BF16), v_ref[0, :, sl]], scale * LOG2E, True)
        z = z_ref[0, :, sl].astype(F32)
        o_ref[0, :, sl] = (o * _silu(z)).astype(o_ref.dtype)


def _attn_sample(h3, k_cache, v_cache, bias_c, bias_n, layer, d_a):
    bsz, t, _ = h3.shape
    ca = k_cache.shape[3]
    n_heads = d_a // DH_A
    assert k_cache.shape[2:] == (n_heads, ca, DH_A)
    row = lambda col: pl.BlockSpec((1, t, d_a), lambda b: (b, 0, col))
    cache = pl.BlockSpec((None, 1, n_heads, ca, DH_A), lambda b: (layer, b, 0, 0, 0))
    bias = lambda a: pl.BlockSpec((n_heads,) + a.shape[1:], lambda b: (layer, 0, 0))
    return pl.pallas_call(
        functools.partial(_attn_sample_kernel, n_heads=n_heads, dh=DH_A, scale=DH_A ** -0.5),
        out_shape=jax.ShapeDtypeStruct((bsz, t, d_a), BF16),
        grid=(bsz,),
        in_specs=[row(0), row(1), row(2), row(3), cache, cache, bias(bias_c), bias(bias_n)],
        out_specs=pl.BlockSpec((1, t, d_a), lambda b: (b, 0, 0)),
        compiler_params=_params("parallel"),
        name="attn_sample",
    )(h3, h3, h3, h3, k_cache, v_cache, bias_c, bias_n)


def _mem_attn_kernel(q_ref, z_ref, mk_ref, mv_ref, o_ref, *, n_heads, dh, scale, head_axis):
    for hd in range(n_heads):
        sl = slice(hd * dh, (hd + 1) * dh)
        mk = mk_ref[0, :, hd, :] if head_axis else mk_ref[0, :, sl]
        mv = mv_ref[0, :, hd, :] if head_axis else mv_ref[0, :, sl]
        s = lax.dot_general(q_ref[0, :, sl], mk.astype(BF16), NT_DIMS, preferred_element_type=F32)
        o = _softmax_pv([s], [mv.astype(BF16)], scale * LOG2E, False)
        z = z_ref[0, :, sl].astype(F32)
        o_ref[0, :, sl] = (o * _silu(z)).astype(o_ref.dtype)


def _mem_attn(h3, mk, mv, mk_spec, mv_spec, head_axis, d_c, q_col0, tq):
    bsz, t, _ = h3.shape
    tq = min(tq, t)
    assert q_col0 % d_c == 0
    qb = q_col0 // d_c
    dh = d_c // H_C
    return pl.pallas_call(
        functools.partial(_mem_attn_kernel, n_heads=H_C, dh=dh, scale=dh ** -0.5, head_axis=head_axis),
        out_shape=jax.ShapeDtypeStruct((bsz, t, d_c), BF16),
        grid=(bsz, t // tq),
        in_specs=[pl.BlockSpec((1, tq, d_c), lambda b, i: (b, i, qb)),
                  pl.BlockSpec((1, tq, d_c), lambda b, i: (b, i, qb + 1)),
                  mk_spec, mv_spec],
        out_specs=pl.BlockSpec((1, tq, d_c), lambda b, i: (b, i, 0)),
        compiler_params=_params("parallel", "arbitrary"),
        name="mem_attn",
    )(h3, h3, mk, mv)


def _seg_cumsum(x, row, group):
    d = 1
    while d < group:
        x = x + jnp.where((row & (group - 1)) >= d, pltpu.roll(x, d, 0), 0.0)
        d *= 2
    return x


def _hgrn_masks():
    t = np.arange(HG_L)[:, None]
    s = np.arange(HG_L)[None, :]
    m16 = [(t // 16 == i) & (s < 16 * i) for i in range(1, 4)]
    m4 = [(t // 16 == s // 16) & ((t % 16) // 4 == i) & (s % 16 < 4 * i) for i in range(1, 4)]
    base = (t // 4 == s // 4) & (s <= t)
    m1 = np.concatenate(m16, axis=1).astype(np.float32)
    m2 = np.concatenate(m4 + [base], axis=1).astype(np.float32)
    return jnp.asarray(m1, BF16), jnp.asarray(m2, BF16)


def _bcast_rows(ref, offsets, reps):
    return jnp.concatenate(
        [jnp.broadcast_to(ref[o:o + 1, :], (reps, LANE)) for o in offsets], axis=0)


def _hgrn_prep(q, g, k, v, cs_ref):
    L = HG_L
    n = q.shape[0] // L
    starts = [c * L for c in range(n)]
    row = lax.broadcasted_iota(jnp.int32, q.shape, 0)
    c4 = _seg_cumsum(g, row, 4)
    c16 = _seg_cumsum(g, row, 16)
    c16_ref, b_ref, carry_ref = cs_ref.at[0], cs_ref.at[1], cs_ref.at[2]
    c16_ref[...] = c16
    ng = L // 16
    for c, s in enumerate(starts):
        tot = jnp.zeros((1, LANE), F32)
        for j in range(ng):
            carry_ref[c * ng + j:c * ng + j + 1, :] = tot
            if j + 1 < ng:
                tot = tot + c16_ref[s + 16 * j + 15:s + 16 * j + 16, :]
    b = c16 + _bcast_rows(carry_ref, list(range(n * ng)), 16)
    b_ref[...] = b

    k16 = []
    for i in range(1, L // 16):
        ref_rows = _bcast_rows(b_ref, [s + 16 * i - 1 for s in starts], L)
        k16.append((k * jnp.exp2(jnp.minimum(ref_rows - b, 0.0))).astype(BF16))
    k4 = []
    for i in range(1, 4):
        ref_rows = _bcast_rows(c16_ref, [s + 16 * j + 4 * i - 1 for s in starts for j in range(L // 16)], 16)
        k4.append((k * jnp.exp2(jnp.minimum(ref_rows - c16, 0.0))).astype(BF16))
    k4.append((k * jnp.exp2(-c4)).astype(BF16))
    return dict(
        starts=starts, b_ref=b_ref, k16=k16, k4=k4, vb=v.astype(BF16),
        q4=(q * jnp.exp2(c4)).astype(BF16), q16=(q * jnp.exp2(c16)).astype(BF16),
        q64=(q * jnp.exp2(b)).astype(BF16),
        kd=(k * jnp.exp2(_bcast_rows(b_ref, [s + L - 1 for s in starts], L) - b)).astype(BF16))


def _hgrn_scores(p, m1, m2):
    scores = []
    for s in p["starts"]:
        sl = slice(s, s + HG_L)
        a1 = lax.dot_general(p["q16"][sl], jnp.concatenate([x[sl] for x in p["k16"]], axis=0), NT_DIMS,
                             preferred_element_type=F32)
        a2 = lax.dot_general(p["q4"][sl], jnp.concatenate([x[sl] for x in p["k4"]], axis=0), NT_DIMS,
                             preferred_element_type=F32)
        scores.append((a1.astype(BF16) * m1, a2.astype(BF16) * m2))
    return scores


def _hgrn_outputs(p, scores, sts, chain_head):
    L = HG_L
    sts = list(sts)
    outs = []
    for c, s in enumerate(p["starts"]):
        sl = slice(s, s + L)
        a1, a2 = scores[c]
        st = sts[chain_head[c]]
        vc = p["vb"][sl]
        outs.append(
            jnp.dot(a1, jnp.concatenate([vc] * 3, axis=0), preferred_element_type=F32)
            + jnp.dot(a2, jnp.concatenate([vc] * 4, axis=0), preferred_element_type=F32)
            + lax.dot_general(p["q64"][sl], st.astype(BF16), NT_DIMS, preferred_element_type=F32))
        sts[chain_head[c]] = (st * jnp.exp2(p["b_ref"][s + L - 1:s + L, :])
                              + lax.dot_general(vc, p["kd"][sl], TN_DIMS, preferred_element_type=F32))
    return jnp.concatenate(outs, axis=0), sts


def _hgrn_unit(q_ref, f_ref, i_ref, z_ref, lb_ref, gn_ref, m1_ref, m2_ref, y_ref, st_ref, cs_ref,
               *, rows, n_sub, unroll):
    m1 = m1_ref[...]
    m2 = m2_ref[...]
    chains = [(u, p) for u in range(unroll) for p in range(HG_HP)]
    chain_head = [p for _, p in chains]
    lb = jnp.concatenate([jnp.broadcast_to(lb_ref[p], (HG_L, LANE)) for _, p in chains], axis=0)
    gn = jnp.concatenate([jnp.broadcast_to(gn_ref[p], (HG_L, LANE)) for _, p in chains], axis=0)

    def stack(ref, r0s):
        parts = []
        for u, p in chains:
            x = ref[0, r0s[u]:r0s[u] + rows, p * LANE:(p + 1) * LANE].astype(F32)
            if rows < HG_L:
                x = jnp.concatenate([x, jnp.zeros((HG_L - rows, LANE), F32)], axis=0)
            parts.append(x)
        return jnp.concatenate(parts, axis=0)

    def finish(pending, sts):
        p, scores, zr, r0s = pending
        o, sts = _hgrn_outputs(p, scores, sts, chain_head)
        on = o * lax.rsqrt(jnp.mean(o * o, axis=-1, keepdims=True) + EPS) * gn
        y = (on * _silu(zr)).astype(y_ref.dtype)
        for c, (u, hp) in enumerate(chains):
            y_ref[0, r0s[u]:r0s[u] + rows, hp * LANE:(hp + 1) * LANE] = y[c * HG_L:c * HG_L + rows]
        return sts

    sts = [st_ref[p] for p in range(HG_HP)]
    pending = None
    for gi in range(n_sub // unroll):
        r0s = [(gi * unroll + u) * rows for u in range(unroll)]
        qr, fr, vr, zr = (stack(ref, r0s) for ref in (q_ref, f_ref, i_ref, z_ref))
        kf = (1.0 - lb) * _sigmoid(-fr)
        g = jnp.log2(1.0 - kf)
        if rows < HG_L:
            live = (lax.broadcasted_iota(jnp.int32, fr.shape, 0) & (HG_L - 1)) < rows
            kf = jnp.where(live, kf, 0.0)
            g = jnp.where(live, g, 0.0)
        p = _hgrn_prep(_silu(qr), g, kf, vr, cs_ref.at[gi])
        yield
        scores = _hgrn_scores(p, m1, m2)
        if pending is not None:
            sts = finish(pending, sts)
        pending = (p, scores, zr, r0s)
        yield
    sts = finish(pending, sts)
    for p in range(HG_HP):
        st_ref[p] = sts[p]


def _hgrn_kernel(q_ref, f_ref, i_ref, z_ref, lb_ref, gn_ref, s0_ref, m1_ref, m2_ref,
                 y_ref, s_out_ref, st_ref, cs_ref, *, rows, n_sub, unroll):
    tb = pl.program_id(2)

    @pl.when(tb == 0)
    def _():
        for p in range(HG_HP):
            st_ref[p] = s0_ref[0, p].astype(F32).T

    for _ in _hgrn_unit(q_ref, f_ref, i_ref, z_ref, lb_ref, gn_ref, m1_ref, m2_ref, y_ref, st_ref, cs_ref,
                        rows=rows, n_sub=n_sub, unroll=unroll):
        pass

    @pl.when(tb == pl.num_programs(2) - 1)
    def _():
        for p in range(HG_HP):
            s_out_ref[0, p] = st_ref[p].T


def _hgrn(h3, lb, gain, s0, layer, masks, col0, d_b):
    bsz, t, _ = h3.shape
    n_heads = d_b // DK_B
    assert col0 % (HG_HP * LANE) == 0 and n_heads % HG_HP == 0 and DK_B == LANE
    assert s0.shape[1:] == (bsz, n_heads, DK_B, LANE)
    if t % HG_L == 0:
        rows, tb = HG_L, min(HG_TB, t)
    else:
        assert t < HG_L and t % 16 == 0
        rows, tb = t, t
    n_sub = tb // rows
    unroll = HG_UNROLL if n_sub % HG_UNROLL == 0 else 1
    assert t % tb == 0
    w = HG_HP * LANE
    c0 = col0 // w
    col = lambda seg: pl.BlockSpec((1, tb, w), lambda b, hp, i: (b, i, c0 + seg * (n_heads // HG_HP) + hp))
    per_head = pl.BlockSpec((HG_HP, 1, LANE), lambda b, hp, i: (hp, 0, 0))
    m1, m2 = masks
    y, s_new = pl.pallas_call(
        functools.partial(_hgrn_kernel, rows=rows, n_sub=n_sub, unroll=unroll),
        out_shape=(jax.ShapeDtypeStruct((bsz, t, d_b), BF16),
                   jax.ShapeDtypeStruct(s0.shape[1:], F32)),
        grid=(bsz, n_heads // HG_HP, t // tb),
        in_specs=[col(0), col(1), col(2), col(3), per_head, per_head,
                  pl.BlockSpec((None, 1, HG_HP, DK_B, LANE), lambda b, hp, i: (layer, b, hp, 0, 0)),
                  pl.BlockSpec(m1.shape, lambda b, hp, i: (0, 0)),
                  pl.BlockSpec(m2.shape, lambda b, hp, i: (0, 0))],
        out_specs=(pl.BlockSpec((1, tb, w), lambda b, hp, i: (b, i, hp)),
                   pl.BlockSpec((1, HG_HP, DK_B, LANE), lambda b, hp, i: (b, hp, 0, 0))),
        scratch_shapes=[pltpu.VMEM((HG_HP, LANE, DK_B), F32),
                        pltpu.VMEM((n_sub // unroll, 3, unroll * HG_HP * HG_L, LANE), F32)],
        compiler_params=_params("parallel", "parallel", "arbitrary"),
        name="hgrn2",
    )(h3, h3, h3, h3, lb.reshape(n_heads, 1, DK_B), gain.reshape(n_heads, 1, LANE).astype(F32),
      s0, m1, m2)
    return y, s_new


def _layer(x, attend, s0, s0_layer, mem_args, norm_g, w_in, lb, hgrn_g, w_branch, w_out, masks, dims, casts=()):
    d_a, d_b, d_c = dims
    bsz, t, d = x.shape
    m = bsz * t
    x2 = x.reshape(m, d)
    cast_out = ()
    if casts:
        h, *cast_out = _norm_matmul(x2, norm_g, w_in, BF16, MM_TM, MM_TN, NORM_RB, casts=casts)
        w_branch, w_out = cast_out[:2]
    else:
        h = _matmul(_rmsnorm(x2, norm_g, BF16, NORM_TM), w_in, 0, BF16, MM_TM, MM_TN)
    h3 = h.reshape(bsz, t, -1)
    ya = attend(h3)
    yb, s_new = _hgrn(h3, lb, hgrn_g, s0, s0_layer, masks, 4 * d_a, d_b)
    yc = _mem_attn(h3, *mem_args, d_c, 4 * d_a + 4 * d_b, MEM_TQ)
    merged = _merge(ya.reshape(m, d_a), yb.reshape(m, d_b), yc.reshape(m, d_c), h, w_branch, 0,
                    4 * d_a + 4 * d_b + 2 * d_c, MM_TM, MM_TN)
    x_new = _matmul(merged, w_out, 0, F32, MM_TM, MM_TN, res=x2).reshape(bsz, t, d)
    ka = h3[:, :, d_a:2 * d_a]
    va = h3[:, :, 2 * d_a:3 * d_a]
    return x_new, ka, va, s_new, tuple(cast_out)


def kernel(x_prompt, x_sample, mem_prompt, cache_attn_k, cache_attn_v, state_hgrn, cache_mem_k, cache_mem_v, norm_gain, w_in, rel_bias, lb_logits, hgrn_norm_gain, mem_norm_gain, w_mem_kv, w_branch, w_out, final_norm_gain):
    depth = w_in.shape[0]
    bp, tp, d = x_prompt.shape
    bs, ts, _ = x_sample.shape
    h_a = rel_bias.shape[1]
    d_a = h_a * DH_A
    d_b = lb_logits.shape[1]
    d_c = w_mem_kv.shape[2] // 2
    h_b = d_b // DK_B
    n_mem = mem_prompt.shape[1]
    dims = (d_a, d_b, d_c)
    ca_s = cache_attn_k.shape[2]
    ca_p = min(BAND, tp)

    w_in_l = w_in[:1].astype(BF16)
    w_mem_l = w_mem_kv[:1].astype(BF16)

    kc_s = jnp.swapaxes(cache_attn_k, 2, 3)
    vc_s = jnp.swapaxes(cache_attn_v, 2, 3)

    lb_all = _lower_bounds(lb_logits)
    masks = _hgrn_masks()
    bias_p = _prompt_bias(rel_bias)
    bias_c, bias_n = _sample_bias(rel_bias, ts, ca_s)
    s0_p = jnp.zeros((1, bp, h_b, DK_B, d_b // h_b), F32)
    mem2 = mem_prompt.reshape(bp * n_mem, d)

    xp, xs = x_prompt, x_sample
    kp_l, vp_l, sp_l, mkp_l, mvp_l, ks_l, vs_l, ss_l = [], [], [], [], [], [], [], []
    for l in range(depth):
        casts = [(w_branch, l, CAST_COLS), (w_out, l, CAST_COLS)]
        if l + 1 < depth:
            casts += [(w_in, l + 1, MM_TN), (w_mem_kv, l + 1, CAST_COLS // 2)]

        memn = _rmsnorm(mem2, mem_norm_gain[l], BF16, NORM_TM)
        mkv = _matmul(memn, w_mem_l, 0, F32, MM_TM, MM_TN).reshape(bp, n_mem, 2 * d_c)
        mem_p = (mkv, mkv,
                 pl.BlockSpec((1, n_mem, d_c), lambda b, i: (b, 0, 0)),
                 pl.BlockSpec((1, n_mem, d_c), lambda b, i: (b, 0, 1)), False)
        att_p = lambda h3: _attn_prompt(h3, bias_p, l, d_a)
        xp, ka, va, s_fin, cast_out = _layer(xp, att_p, s0_p, 0, mem_p, norm_gain[l], w_in_l, lb_all[l],
                                             hgrn_norm_gain[l], None, None, masks, dims, casts=casts)
        w_branch_l, w_out_l = cast_out[:2]
        kp_l.append(ka[:, -ca_p:].astype(F32).reshape(bp, ca_p, h_a, DH_A))
        vp_l.append(va[:, -ca_p:].astype(F32).reshape(bp, ca_p, h_a, DH_A))
        sp_l.append(s_fin)
        mkp_l.append(mkv[:, :, :d_c].reshape(bp, n_mem, H_C, d_c // H_C))
        mvp_l.append(mkv[:, :, d_c:].reshape(bp, n_mem, H_C, d_c // H_C))

        mem_blk = pl.BlockSpec((None, 1, n_mem, H_C, d_c // H_C), lambda b, i: (l, b, 0, 0, 0))
        mem_s = (cache_mem_k, cache_mem_v, mem_blk, mem_blk, True)
        att_s = lambda h3: _attn_sample(h3, kc_s, vc_s, bias_c, bias_n, l, d_a)
        xs, ka_s, va_s, s_new, _ = _layer(xs, att_s, state_hgrn, l, mem_s, norm_gain[l], w_in_l, lb_all[l],
                                          hgrn_norm_gain[l], w_branch_l, w_out_l, masks, dims)
        if l + 1 < depth:
            w_in_l, w_mem_l = cast_out[2:4]
        ks_l.append(ka_s.astype(F32).reshape(bs, ts, h_a, DH_A))
        vs_l.append(va_s.astype(F32).reshape(bs, ts, h_a, DH_A))
        ss_l.append(s_new)

    y_prompt = _rmsnorm(xp.reshape(bp * tp, d), final_norm_gain, F32, NORM_TM).reshape(bp, tp, d)
    y_sample = _rmsnorm(xs.reshape(bs * ts, d), final_norm_gain, F32, NORM_TM).reshape(bs, ts, d)
    return (y_prompt, y_sample,
            jnp.stack(kp_l), jnp.stack(vp_l), jnp.stack(sp_l), jnp.stack(mkp_l), jnp.stack(mvp_l),
            jnp.stack(ks_l), jnp.stack(vs_l), jnp.stack(ss_l))
```
